```python
import math
import jax
import jax.numpy as jnp
from jax import lax
import numpy as np

D_MODEL = 2048
BATCH = 1
SEQ = 8192
DEPTH = 2
DEC_BATCH = 128
DEC_SEQ = 1
PAST_LEN = 16384
PAGE_SIZE = 128

GDN_HEADS = 8
GDN_DK = 128
GDN_DV = 128
GDN_WIDTH = GDN_HEADS * GDN_DV
CONV_W = 4
GDN_CHUNK = 64
NSA_HEADS = 4
NSA_HEAD_DIM = 128
NSA_WIDTH = NSA_HEADS * NSA_HEAD_DIM
CMP_BLOCK = 32
CMP_HIDDEN = 256
SEL_BLOCK = 64
N_SEL = 16
NSA_WINDOW = 512
MLA_HEADS = 4
MLA_Q_LORA = 512
MLA_KV_LORA = 128
MLA_NOPE = 128
MLA_ROPE = 32
MLA_V = 128
MLA_WIDTH = MLA_HEADS * MLA_V
ROPE_THETA = 10000.0
MIX_WIDTH = GDN_WIDTH + NSA_WIDTH + MLA_WIDTH
N_BUCKETS = 32
MAX_DISTANCE = 128
N_GROUPS = 8
EXPERTS_PER_GROUP = 8
N_EXPERTS = N_GROUPS * EXPERTS_PER_GROUP
D_EXPERT = 256
TOP_K_IN_GROUP = 2
MOE_BLOCK = 1024
Q_BLOCK = 128
RMS_EPS = 1e-6
L2_EPS = 1e-6
IN_SIZES = (3 * GDN_WIDTH, GDN_WIDTH, GDN_HEADS, GDN_HEADS, NSA_WIDTH, 6 * NSA_HEAD_DIM, 3 * NSA_HEADS, MLA_Q_LORA, MLA_KV_LORA, MLA_ROPE)

kernel_name = 'hymba_gdn_nsa_mla_hmoe_step'


def rms_norm(x, g):
    xf = x.astype(jnp.float32)
    y = xf * lax.rsqrt(jnp.mean(xf * xf, axis=-1, keepdims=True) + RMS_EPS)
    return (y * g.astype(jnp.float32)).astype(x.dtype)


def l2_normalize(x):
    return x * lax.rsqrt(jnp.sum(x * x, axis=-1, keepdims=True) + L2_EPS)


def masked_softmax(logits, mask):
    l = jnp.where(mask, logits.astype(jnp.float32), -jnp.inf)
    m = jnp.max(l, axis=-1, keepdims=True)
    m = jnp.where(jnp.isfinite(m), m, 0.0)
    p = jnp.where(mask, jnp.exp(l - m), 0.0)
    return p / jnp.maximum(jnp.sum(p, axis=-1, keepdims=True), 1e-30)


def rope(x, pos):
    half = x.shape[-1] // 2
    inv = ROPE_THETA ** (-jnp.arange(half, dtype=jnp.float32) / half)
    ang = pos.astype(jnp.float32)[:, None] * inv
    shape = (pos.shape[0],) + (1,) * (x.ndim - 3) + (half,)
    cos, sin = jnp.cos(ang).reshape(shape), jnp.sin(ang).reshape(shape)
    xf = x.astype(jnp.float32)
    x1, x2 = xf[..., :half], xf[..., half:]
    return jnp.concatenate([x1 * cos - x2 * sin, x1 * sin + x2 * cos], axis=-1).astype(x.dtype)


def sweep_query_blocks(block_fn, n_q):
    qb = Q_BLOCK if n_q % Q_BLOCK == 0 else n_q
    starts = jnp.arange(n_q // qb, dtype=jnp.int32) * qb
    out = lax.map(lambda s: block_fn(s, qb), starts)
    return jnp.moveaxis(out, 0, 1).reshape((out.shape[1], n_q) + out.shape[3:])


def causal_depthwise_conv(x_with_buf, w):
    c = x_with_buf.shape[-1]
    return lax.conv_general_dilated(x_with_buf, w[:, None, :], window_strides=(1,), padding='VALID',
                                    dimension_numbers=('NWC', 'WIO', 'NWC'), feature_group_count=c)


def gated_delta_chunked(q, k, v, beta, g, s0):
    bsz, t_len, nh, dk = q.shape
    dv = v.shape[-1]
    c = GDN_CHUNK if t_len % GDN_CHUNK == 0 else t_len
    n = t_len // c

    def chunks(a):
        a = a.reshape((bsz, n, c, nh) + a.shape[3:])
        return jnp.moveaxis(jnp.swapaxes(a, 2, 3), 1, 0)

    q = chunks(q) * (dk ** -0.5)
    k, v, beta, g = chunks(k), chunks(v), chunks(beta), chunks(g)
    gc = jnp.cumsum(g, axis=-1)
    causal = jnp.tril(jnp.ones((c, c), dtype=bool))
    strict = jnp.tril(jnp.ones((c, c), dtype=bool), -1)
    decay = jnp.exp(jnp.where(causal, gc[..., :, None] - gc[..., None, :], -jnp.inf))
    kb = k * beta[..., None]
    lower = jnp.where(strict, jnp.einsum('nbhid,nbhjd->nbhij', kb, k) * decay, 0.0)
    eye = jnp.eye(c, dtype=jnp.float32)
    tmat = lax.linalg.triangular_solve(eye + lower, jnp.broadcast_to(eye, lower.shape),
                                       left_side=True, lower=True, unit_diagonal=True)
    u = tmat @ (v * beta[..., None])
    w = tmat @ (kb * jnp.exp(gc)[..., None])
    attn = jnp.einsum('nbhid,nbhjd->nbhij', q, k) * decay
    qg = q * jnp.exp(gc)[..., None]
    kd = k * jnp.exp(gc[..., -1:] - gc)[..., None]
    glast = jnp.exp(gc[..., -1])[..., None, None]

    def step(state, xs):
        u_c, w_c, a_c, qg_c, kd_c, gl_c = xs
        v_new = u_c - w_c @ state
        o = qg_c @ state + a_c @ v_new
        state = state * gl_c + jnp.swapaxes(kd_c, -1, -2) @ v_new
        return state, o

    s_fin, o = lax.scan(step, s0, (u, w, attn, qg, kd, glast))
    o = jnp.swapaxes(jnp.moveaxis(o, 0, 1), 2, 3).reshape(bsz, t_len, nh, dv)
    return o, s_fin


def t5_bucket(dist):
    n = jnp.maximum(dist, 0)
    exact = N_BUCKETS // 2
    nf = jnp.maximum(n, exact).astype(jnp.float32)
    large = exact + (jnp.log(nf / exact) / math.log(MAX_DISTANCE / exact) * (N_BUCKETS - exact)).astype(jnp.int32)
    return jnp.where(n < exact, n, jnp.minimum(large, N_BUCKETS - 1))


def rel_bias_lookup(table, dist):
    return jnp.moveaxis(table[t5_bucket(dist)].astype(jnp.float32), -1, -3)


def nsa_compress(rows, pe, w1, w2):
    bsz, s, dh = rows.shape
    nc = s // CMP_BLOCK
    blk = rows.reshape(bsz, nc, CMP_BLOCK, dh) + pe
    return jax.nn.gelu(blk.reshape(bsz, nc, CMP_BLOCK * dh) @ w1) @ w2


def nsa_attend(q, gates, pos, cmp_all, sel_all, win_all, win_offset, rel_bias, k_norm_cmp, pe, w1, w2):
    bsz, tq, nh, dh = q.shape
    s_pad = cmp_all.shape[1]
    nc, ns = s_pad // CMP_BLOCK, s_pad // SEL_BLOCK
    n_sel = min(N_SEL, ns)
    scale = dh ** -0.5
    kcmp = rms_norm(nsa_compress(cmp_all[..., :dh], pe[0], w1[0], w2[0]), k_norm_cmp)
    vcmp = nsa_compress(cmp_all[..., dh:], pe[1], w1[1], w2[1])
    cmp_end = jnp.arange(nc, dtype=jnp.int32) * CMP_BLOCK + (CMP_BLOCK - 1)
    blk_ids = jnp.arange(ns, dtype=jnp.int32)
    sel_offsets = jnp.arange(SEL_BLOCK, dtype=jnp.int32)
    win_pad = jnp.pad(win_all, ((0, 0), (NSA_WINDOW, 0), (0, 0)))

    def block(s, qb):
        qblk = lax.dynamic_slice_in_dim(q, s, qb, 1)
        t = lax.dynamic_slice_in_dim(pos, s, qb, 0)
        gblk = lax.dynamic_slice_in_dim(gates, s, qb, 1)
        dist_c = t[:, None] - cmp_end[None, :]
        lc = jnp.einsum('bqhd,bcd->bhqc', qblk, kcmp).astype(jnp.float32) * scale + rel_bias_lookup(rel_bias, dist_c)
        pc = masked_softmax(lc, dist_c >= 0)
        o_c = jnp.einsum('bhqc,bcd->bqhd', pc.astype(q.dtype), vcmp)
        imp = pc.sum(axis=1).reshape(bsz, qb, ns, SEL_BLOCK // CMP_BLOCK).sum(-1)
        cur = (t // SEL_BLOCK)[:, None]
        forced = (blk_ids == 0) | (blk_ids == cur) | (blk_ids == cur - 1)
        valid = blk_ids * SEL_BLOCK <= t[:, None]
        imp = jnp.where(forced, jnp.inf, jnp.where(valid, imp, -1.0))
        _, sel = lax.top_k(imp, n_sel)
        kidx = (sel[..., None] * SEL_BLOCK + sel_offsets).reshape(bsz, qb, n_sel * SEL_BLOCK)
        kv_sel = jax.vmap(lambda rows, idx: rows[idx])(sel_all, kidx)
        dist_s = t[None, :, None] - kidx
        ls = jnp.einsum('bqhd,bqkd->bhqk', qblk, kv_sel[..., :dh]).astype(jnp.float32) * scale + rel_bias_lookup(rel_bias, dist_s)
        ps = masked_softmax(ls, (dist_s >= 0)[:, None])
        o_s = jnp.einsum('bhqk,bqkd->bqhd', ps.astype(q.dtype), kv_sel[..., dh:])
        t0 = t[0]
        kv_w = lax.dynamic_slice_in_dim(win_pad, t0 - win_offset, qb + NSA_WINDOW, 1)
        kpos = t0 - NSA_WINDOW + jnp.arange(qb + NSA_WINDOW, dtype=jnp.int32)
        dist_w = t[:, None] - kpos[None, :]
        mask_w = (dist_w >= 0) & (dist_w <= NSA_WINDOW) & (kpos[None, :] >= 0)
        lw_ = jnp.einsum('bqhd,bkd->bhqk', qblk, kv_w[..., :dh]).astype(jnp.float32) * scale + rel_bias_lookup(rel_bias, dist_w)
        pw = masked_softmax(lw_, mask_w)
        o_w = jnp.einsum('bhqk,bkd->bqhd', pw.astype(q.dtype), kv_w[..., dh:])
        return gblk[..., 0:1] * o_c + gblk[..., 1:2] * o_s + gblk[..., 2:3] * o_w

    return sweep_query_blocks(block, tq)


def mla_attend(q_lat, q_rope, ckv, kr, pos):
    kpos = jnp.arange(ckv.shape[1], dtype=jnp.int32)
    scale = (MLA_NOPE + MLA_ROPE) ** -0.5

    def block(s, qb):
        ql = lax.dynamic_slice_in_dim(q_lat, s, qb, 1)
        qr = lax.dynamic_slice_in_dim(q_rope, s, qb, 1)
        t = lax.dynamic_slice_in_dim(pos, s, qb, 0)
        logits = (jnp.einsum('bqhc,bsc->bhqs', ql, ckv) + jnp.einsum('bqhr,bsr->bhqs', qr, kr)).astype(jnp.float32) * scale
        p = masked_softmax(logits, kpos[None, :] <= t[:, None])
        return jnp.einsum('bhqs,bsc->bqhc', p.astype(ckv.dtype), ckv)

    return sweep_query_blocks(block, q_lat.shape[1])


def hier_moe(h, w_grp, b_grp, w_exp, b_exp, w_gate, w_up, w_down):
    n, d = h.shape
    hf = h.astype(jnp.float32)
    p_grp = jax.nn.softmax(hf @ w_grp.astype(jnp.float32) + b_grp.astype(jnp.float32), axis=-1)
    p_top, g_idx = lax.top_k(p_grp, 1)
    le = (hf @ w_exp.astype(jnp.float32) + b_exp.astype(jnp.float32)).reshape(n, N_GROUPS, EXPERTS_PER_GROUP)
    le = jnp.take_along_axis(le, g_idx[:, :, None], axis=1)[:, 0]
    lv, li = lax.top_k(le, TOP_K_IN_GROUP)
    wts = p_top * jax.nn.softmax(lv, axis=-1)
    eid = g_idx * EXPERTS_PER_GROUP + li
    combine = jnp.sum(jax.nn.one_hot(eid, N_EXPERTS, dtype=jnp.float32) * wts[..., None], axis=1).astype(h.dtype)
    nb = n // MOE_BLOCK if n % MOE_BLOCK == 0 else 1

    def expert_block(args):
        hb, cb = args
        a = jax.nn.silu(jnp.einsum('nd,edf->nef', hb, w_gate)) * jnp.einsum('nd,edf->nef', hb, w_up)
        return jnp.einsum('nef,efd->nd', a * cb[..., None], w_down)

    y = lax.map(expert_block, (h.reshape(nb, n // nb, d), combine.reshape(nb, n // nb, N_EXPERTS)))
    return y.reshape(n, d)


def hybrid_layer(x, pos0, past_cmp, past_sel, past_mla, win_buf, gdn_state, conv_buf, rel_bias, lw):
    bsz, t_len, _ = x.shape
    pos = pos0 + jnp.arange(t_len, dtype=jnp.int32)
    h = rms_norm(x, lw['norm1'])
    split_points = np.cumsum(np.array(IN_SIZES))[:-1].tolist()
    (qkv_raw, z, b_raw, a_raw, nsa_q, nsa_kv, nsa_g, mla_cq, mla_ckv, mla_kr) = jnp.split(h @ lw['w_in'], split_points, axis=-1)

    conv_in = jnp.concatenate([conv_buf.astype(x.dtype), qkv_raw], axis=1)
    new_conv = conv_in[:, conv_in.shape[1] - (CONV_W - 1):]
    qkv = jax.nn.silu(causal_depthwise_conv(conv_in, lw['gdn_conv']).astype(jnp.float32))
    gq, gk, gv = jnp.split(qkv, 3, axis=-1)
    gq = l2_normalize(gq.reshape(bsz, t_len, GDN_HEADS, GDN_DK))
    gk = l2_normalize(gk.reshape(bsz, t_len, GDN_HEADS, GDN_DK))
    gv = gv.reshape(bsz, t_len, GDN_HEADS, GDN_DV)
    beta = jax.nn.sigmoid(b_raw.astype(jnp.float32))
    g = -jnp.exp(lw['gdn_a_log'].astype(jnp.float32)) * jax.nn.softplus(a_raw.astype(jnp.float32) + lw['gdn_dt_bias'].astype(jnp.float32))
    o_gdn, new_gdn = gated_delta_chunked(gq, gk, gv, beta, g, gdn_state.astype(jnp.float32))
    o_gdn = rms_norm(o_gdn, lw['gdn_out_norm']) * jax.nn.silu(z.astype(jnp.float32).reshape(bsz, t_len, GDN_HEADS, GDN_DV))
    o_gdn = o_gdn.reshape(bsz, t_len, GDN_WIDTH).astype(x.dtype)

    qn = rms_norm(nsa_q.reshape(bsz, t_len, NSA_HEADS, NSA_HEAD_DIM), lw['nsa_q_norm'])
    kc, vc, ksel, vsel, kw, vw = jnp.split(nsa_kv, 6, axis=-1)
    new_cmp = jnp.concatenate([kc, vc], axis=-1)
    new_sel = jnp.concatenate([rms_norm(ksel, lw['nsa_k_norm'][1]), vsel], axis=-1)
    new_win_rows = jnp.concatenate([rms_norm(kw, lw['nsa_k_norm'][2]), vw], axis=-1)
    n_keys = past_cmp.shape[1] + t_len
    zpad = jnp.zeros((bsz, (-n_keys) % SEL_BLOCK, 2 * NSA_HEAD_DIM), x.dtype)
    cmp_all = jnp.concatenate([past_cmp.astype(x.dtype), new_cmp, zpad], axis=1)
    sel_all = jnp.concatenate([past_sel.astype(x.dtype), new_sel, zpad], axis=1)
    win_all = jnp.concatenate([win_buf.astype(x.dtype), new_win_rows], axis=1)
    keep = min(NSA_WINDOW, win_all.shape[1])
    new_win = win_all[:, win_all.shape[1] - keep:]
    gates = jax.nn.sigmoid(nsa_g.astype(jnp.float32)).reshape(bsz, t_len, NSA_HEADS, 3).astype(x.dtype)
    o_nsa = nsa_attend(qn, gates, pos, cmp_all, sel_all, win_all, pos0 - win_buf.shape[1], rel_bias,
                       lw['nsa_k_norm'][0], lw['nsa_cmp_pe'], lw['nsa_cmp_w1'], lw['nsa_cmp_w2'])
    o_nsa = o_nsa.reshape(bsz, t_len, NSA_WIDTH)

    cq = rms_norm(mla_cq, lw['mla_q_a_norm'])
    qf = rms_norm((cq @ lw['mla_w_uq']).reshape(bsz, t_len, MLA_HEADS, MLA_NOPE + MLA_ROPE), lw['mla_qk_norm'])
    q_rope = rope(qf[..., MLA_NOPE:], pos)
    q_lat = jnp.einsum('bthn,chn->bthc', qf[..., :MLA_NOPE], lw['mla_w_uk'])
    new_mla = jnp.concatenate([rms_norm(mla_ckv, lw['mla_kv_norm']), rope(rms_norm(mla_kr, lw['mla_krope_norm']), pos)], axis=-1)
    mla_all = jnp.concatenate([past_mla.astype(x.dtype), new_mla], axis=1)
    o_lat = mla_attend(q_lat, q_rope, mla_all[..., :MLA_KV_LORA], mla_all[..., MLA_KV_LORA:], pos)
    o_mla = jnp.einsum('bthc,chv->bthv', o_lat, lw['mla_w_uv']).reshape(bsz, t_len, MLA_WIDTH)

    x = x + jnp.concatenate([o_gdn, o_nsa, o_mla], axis=-1) @ lw['w_out']
    h2 = rms_norm(x, lw['norm2']).reshape(bsz * t_len, D_MODEL)
    x = x + hier_moe(h2, lw['moe_w_grp'], lw['moe_b_grp'], lw['moe_w_exp'], lw['moe_b_exp'],
                     lw['moe_w_gate'], lw['moe_w_up'], lw['moe_w_down']).reshape(bsz, t_len, D_MODEL)
    return x, new_cmp, new_sel, new_mla, new_win, new_gdn, new_conv


def setup_inputs(seed: int = 0) -> dict:
    key = jax.random.key(seed)
    keys = iter(jax.random.split(key, 64))

    def normal(shape, scale):
        return jax.random.normal(next(keys), shape, jnp.float32) * scale

    def gain(shape):
        return 1.0 + 0.01 * jax.random.normal(next(keys), shape, jnp.float32)

    n_pages = PAST_LEN // PAGE_SIZE
    n_used = DEC_BATCH * n_pages
    n_pool = n_used + max(1, n_used // 4)
    win_rows = min(NSA_WINDOW, PAST_LEN)
    perm = jax.random.permutation(next(keys), n_pool)
    page_table = perm[:n_used].reshape(DEC_BATCH, n_pages).astype(jnp.int32)
    n_in = sum(IN_SIZES)
    a_log = jnp.log(jax.random.uniform(next(keys), (DEPTH, GDN_HEADS), jnp.float32, 1.0, 16.0))
    dt = jnp.exp(jax.random.uniform(next(keys), (DEPTH, GDN_HEADS), jnp.float32, math.log(1e-3), math.log(1e-1)))
    dt_bias = dt + jnp.log(-jnp.expm1(-dt))
    return {
        'x_prompt': normal((BATCH, SEQ, D_MODEL), 1.0),
        'x_sample': normal((DEC_BATCH, DEC_SEQ, D_MODEL), 1.0),
        'cache_nsa_cmp': normal((n_pool, DEPTH, PAGE_SIZE, 2 * NSA_HEAD_DIM), 1.0),
        'cache_nsa_sel': normal((n_pool, DEPTH, PAGE_SIZE, 2 * NSA_HEAD_DIM), 1.0),
        'cache_mla': normal((n_pool, DEPTH, PAGE_SIZE, MLA_KV_LORA + MLA_ROPE), 1.0),
        'state_win_kv': normal((DEC_BATCH, DEPTH, win_rows, 2 * NSA_HEAD_DIM), 1.0),
        'state_gdn': normal((DEC_BATCH, DEPTH, GDN_HEADS, GDN_DK, GDN_DV), 0.05),
        'state_conv': normal((DEC_BATCH, DEPTH, CONV_W - 1, 3 * GDN_WIDTH), 1.0),
        'page_table': page_table,
        'rel_bias': normal((N_BUCKETS, NSA_HEADS), 0.5),
        'norm1': gain((DEPTH, D_MODEL)),
        'w_in': normal((DEPTH, D_MODEL, n_in), D_MODEL ** -0.5),
        'gdn_conv': normal((DEPTH, CONV_W, 3 * GDN_WIDTH), CONV_W ** -0.5),
        'gdn_a_log': a_log,
        'gdn_dt_bias': dt_bias,
        'gdn_out_norm': gain((DEPTH, GDN_DV)),
        'nsa_q_norm': gain((DEPTH, NSA_HEAD_DIM)),
        'nsa_k_norm': gain((DEPTH, 3, NSA_HEAD_DIM)),
        'nsa_cmp_pe': normal((DEPTH, 2, CMP_BLOCK, NSA_HEAD_DIM), 0.1),
        'nsa_cmp_w1': normal((DEPTH, 2, CMP_BLOCK * NSA_HEAD_DIM, CMP_HIDDEN), (CMP_BLOCK * NSA_HEAD_DIM) ** -0.5),
        'nsa_cmp_w2': normal((DEPTH, 2, CMP_HIDDEN, NSA_HEAD_DIM), CMP_HIDDEN ** -0.5),
        'mla_q_a_norm': gain((DEPTH, MLA_Q_LORA)),
        'mla_w_uq': normal((DEPTH, MLA_Q_LORA, MLA_HEADS * (MLA_NOPE + MLA_ROPE)), MLA_Q_LORA ** -0.5),
        'mla_qk_norm': gain((DEPTH, MLA_NOPE + MLA_ROPE)),
        'mla_kv_norm': gain((DEPTH, MLA_KV_LORA)),
        'mla_krope_norm': gain((DEPTH, MLA_ROPE)),
        'mla_w_uk': normal((DEPTH, MLA_KV_LORA, MLA_HEADS, MLA_NOPE), MLA_KV_LORA ** -0.5),
        'mla_w_uv': normal((DEPTH, MLA_KV_LORA, MLA_HEADS, MLA_V), MLA_KV_LORA ** -0.5),
        'w_out': normal((DEPTH, MIX_WIDTH, D_MODEL), MIX_WIDTH ** -0.5),
        'norm2': gain((DEPTH, D_MODEL)),
        'moe_w_grp': normal((DEPTH, D_MODEL, N_GROUPS), D_MODEL ** -0.5),
        'moe_b_grp': normal((DEPTH, N_GROUPS), 0.01),
        'moe_w_exp': normal((DEPTH, D_MODEL, N_EXPERTS), D_MODEL ** -0.5),
        'moe_b_exp': normal((DEPTH, N_EXPERTS), 0.01),
        'moe_w_gate': normal((DEPTH, N_EXPERTS, D_MODEL, D_EXPERT), D_MODEL ** -0.5),
        'moe_w_up': normal((DEPTH, N_EXPERTS, D_MODEL, D_EXPERT), D_MODEL ** -0.5),
        'moe_w_down': normal((DEPTH, N_EXPERTS, D_EXPERT, D_MODEL), D_EXPERT ** -0.5),
    }


def reference(x_prompt, x_sample, cache_nsa_cmp, cache_nsa_sel, cache_mla, state_win_kv, state_gdn, state_conv,
              page_table, rel_bias, norm1, w_in, gdn_conv, gdn_a_log, gdn_dt_bias, gdn_out_norm, nsa_q_norm,
              nsa_k_norm, nsa_cmp_pe, nsa_cmp_w1, nsa_cmp_w2, mla_q_a_norm, mla_w_uq, mla_qk_norm, mla_kv_norm,
              mla_krope_norm, mla_w_uk, mla_w_uv, w_out, norm2, moe_w_grp, moe_b_grp, moe_w_exp, moe_b_exp,
              moe_w_gate, moe_w_up, moe_w_down):
    def layer_weights(l):
        return dict(norm1=norm1[l], w_in=w_in[l], gdn_conv=gdn_conv[l], gdn_a_log=gdn_a_log[l],
                    gdn_dt_bias=gdn_dt_bias[l], gdn_out_norm=gdn_out_norm[l], nsa_q_norm=nsa_q_norm[l],
                    nsa_k_norm=nsa_k_norm[l], nsa_cmp_pe=nsa_cmp_pe[l], nsa_cmp_w1=nsa_cmp_w1[l],
                    nsa_cmp_w2=nsa_cmp_w2[l], mla_q_a_norm=mla_q_a_norm[l], mla_w_uq=mla_w_uq[l],
                    mla_qk_norm=mla_qk_norm[l], mla_kv_norm=mla_kv_norm[l], mla_krope_norm=mla_krope_norm[l],
                    mla_w_uk=mla_w_uk[l], mla_w_uv=mla_w_uv[l], w_out=w_out[l], norm2=norm2[l],
                    moe_w_grp=moe_w_grp[l], moe_b_grp=moe_b_grp[l], moe_w_exp=moe_w_exp[l],
                    moe_b_exp=moe_b_exp[l], moe_w_gate=moe_w_gate[l], moe_w_up=moe_w_up[l],
                    moe_w_down=moe_w_down[l])

    bp, dtp = x_prompt.shape[0], x_prompt.dtype
    xp = x_prompt
    p_cmp, p_sel, p_mla, p_win, p_gdn, p_conv = [], [], [], [], [], []
    for l in range(DEPTH):
        xp, c_rows, s_rows, m_rows, w_rows, g_st, cv_st = hybrid_layer(
            xp, 0,
            jnp.zeros((bp, 0, 2 * NSA_HEAD_DIM), dtp), jnp.zeros((bp, 0, 2 * NSA_HEAD_DIM), dtp),
            jnp.zeros((bp, 0, MLA_KV_LORA + MLA_ROPE), dtp), jnp.zeros((bp, 0, 2 * NSA_HEAD_DIM), dtp),
            jnp.zeros((bp, GDN_HEADS, GDN_DK, GDN_DV), jnp.float32), jnp.zeros((bp, CONV_W - 1, 3 * GDN_WIDTH), dtp),
            rel_bias, layer_weights(l))
        p_cmp.append(c_rows); p_sel.append(s_rows); p_mla.append(m_rows)
        p_win.append(w_rows); p_gdn.append(g_st); p_conv.append(cv_st)

    bs = x_sample.shape[0]
    past = page_table.shape[1] * PAGE_SIZE
    xs = x_sample
    s_cmp, s_sel, s_mla, s_win, s_gdn, s_conv = [], [], [], [], [], []
    for l in range(DEPTH):
        past_cmp = cache_nsa_cmp[page_table, l].reshape(bs, past, 2 * NSA_HEAD_DIM)
        past_sel = cache_nsa_sel[page_table, l].reshape(bs, past, 2 * NSA_HEAD_DIM)
        past_mla = cache_mla[page_table, l].reshape(bs, past, MLA_KV_LORA + MLA_ROPE)
        xs, c_rows, s_rows, m_rows, w_rows, g_st, cv_st = hybrid_layer(
            xs, past, past_cmp, past_sel, past_mla, state_win_kv[:, l], state_gdn[:, l], state_conv[:, l],
            rel_bias, layer_weights(l))
        s_cmp.append(c_rows); s_sel.append(s_rows); s_mla.append(m_rows)
        s_win.append(w_rows); s_gdn.append(g_st); s_conv.append(cv_st)

    return (xp, xs,
            jnp.stack(p_cmp, axis=1), jnp.stack(p_sel, axis=1), jnp.stack(p_mla, axis=1),
            jnp.stack(p_win, axis=1), jnp.stack(p_gdn, axis=1), jnp.stack(p_conv, axis=1),
            jnp.stack(s_cmp, axis=1), jnp.stack(s_sel, axis=1), jnp.stack(s_mla, axis=1),
            jnp.stack(s_win, axis=1), jnp.stack(s_gdn, axis=1), jnp.stack(s_conv, axis=1))
```

```python
import functools
import math

import jax
import jax.numpy as jnp
import numpy as np
from jax import lax
from jax.experimental import pallas as pl
from jax.experimental.pallas import tpu as pltpu

F32, BF16, I32 = jnp.float32, jnp.bfloat16, jnp.int32
HI = lax.Precision.HIGHEST

D_MODEL = 2048
PAGE_SIZE = 128
GDN_HEADS, GDN_DK, GDN_DV = 8, 128, 128
GDN_WIDTH = GDN_HEADS * GDN_DV
CONV_W = 4
GDN_CHUNK = 64
NSA_HEADS, NSA_HEAD_DIM = 4, 128
NSA_WIDTH = NSA_HEADS * NSA_HEAD_DIM
CMP_BLOCK, CMP_HIDDEN, SEL_BLOCK, N_SEL, NSA_WINDOW = 32, 256, 64, 16, 512
MLA_HEADS, MLA_Q_LORA, MLA_KV_LORA, MLA_NOPE, MLA_ROPE, MLA_V = 4, 512, 128, 128, 32, 128
MLA_QK = MLA_NOPE + MLA_ROPE
MLA_LAT = MLA_KV_LORA + MLA_ROPE
ROPE_THETA = 10000.0
N_BUCKETS, MAX_DISTANCE = 32, 128
N_GROUPS, EXPERTS_PER_GROUP = 8, 8
N_EXPERTS = N_GROUPS * EXPERTS_PER_GROUP
D_EXPERT = 256
RMS_EPS = 1e-6
L2_EPS = 1e-6
Q_TILE = 128
K_TILE = 256
LANES = 128
VMEM_LIMIT = 56 << 20


def _cparams(sem, vmem=VMEM_LIMIT):
    return pltpu.CompilerParams(dimension_semantics=sem, vmem_limit_bytes=vmem)


def _bdot(a, b):
    return jnp.dot(a.astype(BF16), b.astype(BF16), preferred_element_type=F32)


def _bdot_nt(a, b):
    return lax.dot_general(a.astype(BF16), b.astype(BF16), (((1,), (1,)), ((), ())), preferred_element_type=F32)


def _rms(x, g, n=None):
    n = x.shape[-1] if n is None else n
    return x * lax.rsqrt(jnp.sum(x * x, axis=-1, keepdims=True) / n + RMS_EPS) * g


def _sigmoid(x):
    return 1.0 / (1.0 + jnp.exp(-x))


def _t5_thresholds():
    n = np.arange(0, 4 * MAX_DISTANCE)
    exact = N_BUCKETS // 2
    out = []
    for dt in (np.float32, np.float64):
        nf = np.maximum(n, exact).astype(dt)
        large = exact + (np.log(nf / exact) / math.log(MAX_DISTANCE / exact) * (N_BUCKETS - exact)).astype(np.int32)
        b = np.where(n < exact, n, np.minimum(large, N_BUCKETS - 1))
        out.append([int(np.argmax(b >= k)) for k in range(1, N_BUCKETS)])
    assert out[0] == out[1]
    return out[0]


T5_THR = _t5_thresholds()


def _t5_bias(dist, tbl_ref, h):
    acc = jnp.full(dist.shape, tbl_ref[0, h], F32)
    for b in range(1, N_BUCKETS):
        acc = acc + jnp.where(dist >= T5_THR[b - 1], tbl_ref[b, h] - tbl_ref[b - 1, h], 0.0)
    return acc


def _norm_matmul(x, g, w, tm, tn):
    m, d = x.shape
    n = w.shape[1]

    def body(x_ref, g_ref, w_ref, o_ref, h_ref):
        @pl.when(pl.program_id(1) == 0)
        def _():
            h_ref[...] = _rms(x_ref[...], g_ref[...]).astype(BF16)

        o_ref[...] = jnp.dot(h_ref[...], w_ref[...], preferred_element_type=F32)

    return pl.pallas_call(
        body, grid=(m // tm, n // tn),
        in_specs=[pl.BlockSpec((tm, d), lambda i, j: (i, 0)), pl.BlockSpec((1, d), lambda i, j: (0, 0)),
                  pl.BlockSpec((d, tn), lambda i, j: (0, j))],
        out_specs=pl.BlockSpec((tm, tn), lambda i, j: (i, j)),
        out_shape=jax.ShapeDtypeStruct((m, n), F32),
        scratch_shapes=[pltpu.VMEM((tm, d), BF16)],
        compiler_params=_cparams(("arbitrary", "arbitrary")), name="norm_matmul")(x, g, w)


def _out_proj(x, a1, a2, a3, w, tm, tn):
    m, d = x.shape
    k1, k2, k3 = a1.shape[1], a2.shape[1], a3.shape[1]
    assert k1 % k2 == 0 and k2 == k3

    def body(x_ref, a1_ref, a2_ref, a3_ref, w1_ref, w2_ref, w3_ref, o_ref):
        o_ref[...] = (x_ref[...] + _bdot(a1_ref[...], w1_ref[...]) + _bdot(a2_ref[...], w2_ref[...])
                      + _bdot(a3_ref[...], w3_ref[...]))

    return pl.pallas_call(
        body, grid=(m // tm, d // tn),
        in_specs=[pl.BlockSpec((tm, tn), lambda i, j: (i, j)),
                  pl.BlockSpec((tm, k1), lambda i, j: (i, 0)), pl.BlockSpec((tm, k2), lambda i, j: (i, 0)),
                  pl.BlockSpec((tm, k3), lambda i, j: (i, 0)),
                  pl.BlockSpec((k1, tn), lambda i, j: (0, j)),
                  pl.BlockSpec((k2, tn), lambda i, j: (k1 // k2, j)),
                  pl.BlockSpec((k3, tn), lambda i, j: (k1 // k2 + 1, j))],
        out_specs=pl.BlockSpec((tm, tn), lambda i, j: (i, j)),
        out_shape=jax.ShapeDtypeStruct((m, d), F32),
        compiler_params=_cparams(("arbitrary", "arbitrary")), name="out_proj")(x, a1, a2, a3, w, w, w)


def _headwise_mm(x, w):
    h, m, k = x.shape
    n = w.shape[2]
    tm = min(m, 512)

    def body(x_ref, w_ref, o_ref):
        for i in range(h):
            o_ref[:, i * n:(i + 1) * n] = _bdot(x_ref[i], w_ref[i])

    return pl.pallas_call(
        body, grid=(m // tm,),
        in_specs=[pl.BlockSpec((h, tm, k), lambda i: (0, i, 0)), pl.BlockSpec((h, k, n), lambda i: (0, 0, 0))],
        out_specs=pl.BlockSpec((tm, h * n), lambda i: (i, 0)),
        out_shape=jax.ShapeDtypeStruct((m, h * n), F32),
        compiler_params=_cparams(("arbitrary",)), name="headwise_mm")(x, w)


def _moe_router(x, g, wr, br, tm):
    m, d = x.shape

    def body(x_ref, g_ref, wr_ref, br_ref, h_ref, r_ref):
        h = _rms(x_ref[...], g_ref[...])
        h_ref[...] = h.astype(BF16)
        lg = jnp.dot(h, wr_ref[...], precision=HI, preferred_element_type=F32) + br_ref[...]
        lane = lax.broadcasted_iota(I32, lg.shape, 1)
        is_g = lane < N_GROUPS
        lgm = jnp.where(is_g, lg, -jnp.inf)
        mg = jnp.max(lgm, axis=-1, keepdims=True)
        p_top = 1.0 / jnp.sum(jnp.where(is_g, jnp.exp(lgm - mg), 0.0), axis=-1, keepdims=True)
        gidx = jnp.min(jnp.where(lgm == mg, lane, LANES), axis=-1, keepdims=True)
        in_grp = (lane >= N_GROUPS) & (lane < N_GROUPS + N_EXPERTS) & (((lane - N_GROUPS) // EXPERTS_PER_GROUP) == gidx)
        le = jnp.where(in_grp, lg, -jnp.inf)
        m1 = jnp.max(le, axis=-1, keepdims=True)
        i1 = jnp.min(jnp.where(le == m1, lane, LANES), axis=-1, keepdims=True)
        le2 = jnp.where(lane == i1, -jnp.inf, le)
        m2 = jnp.max(le2, axis=-1, keepdims=True)
        i2 = jnp.min(jnp.where(le2 == m2, lane, LANES), axis=-1, keepdims=True)
        e2 = jnp.exp(m2 - m1)
        w1 = p_top / (1.0 + e2)
        w2 = p_top * e2 / (1.0 + e2)
        r_ref[...] = jnp.where(lane == 0, (i1 - N_GROUPS).astype(F32),
                               jnp.where(lane == 1, (i2 - N_GROUPS).astype(F32),
                                         jnp.where(lane == 2, w1, jnp.where(lane == 3, w2, 0.0))))

    return pl.pallas_call(
        body, grid=(m // tm,),
        in_specs=[pl.BlockSpec((tm, d), lambda i: (i, 0)), pl.BlockSpec((1, d), lambda i: (0, 0)),
                  pl.BlockSpec((d, LANES), lambda i: (0, 0)), pl.BlockSpec((1, LANES), lambda i: (0, 0))],
        out_specs=[pl.BlockSpec((tm, d), lambda i: (i, 0)), pl.BlockSpec((tm, LANES), lambda i: (i, 0))],
        out_shape=[jax.ShapeDtypeStruct((m, d), BF16), jax.ShapeDtypeStruct((m, LANES), F32)],
        compiler_params=_cparams(("arbitrary",)), name="moe_router")(x, g, wr, br)


def _moe_dispatch(eid, wts, tm):
    t = eid.shape[0]
    n = 2 * t
    n_tiles = n // tm + N_EXPERTS
    e_flat = eid.reshape(n)
    order = jnp.argsort(e_flat, stable=True)
    e_s = e_flat[order]
    counts = jnp.sum(jax.nn.one_hot(e_flat, N_EXPERTS, dtype=I32), axis=0)
    tiles_e = (counts + tm - 1) // tm
    tile_end = jnp.cumsum(tiles_e)
    grp_start = jnp.cumsum(counts) - counts
    dest_sorted = (tile_end - tiles_e)[e_s] * tm + (jnp.arange(n, dtype=I32) - grp_start[e_s])
    row_src = jnp.zeros((n_tiles * tm,), I32).at[dest_sorted].set((order // 2).astype(I32))
    row_w = jnp.zeros((n_tiles * tm,), F32).at[dest_sorted].set(wts.reshape(n)[order])
    slot_dest = jnp.zeros((n,), I32).at[order].set(dest_sorted).reshape(t, 2)
    n_valid = tile_end[-1]
    ti = jnp.arange(n_tiles, dtype=I32)
    tile_e = jnp.minimum(jnp.searchsorted(tile_end, ti, side="right").astype(I32), N_EXPERTS - 1)
    tile_e = jnp.where(ti < n_valid, tile_e, tile_e[n_valid - 1])
    return row_src, row_w, slot_dest, tile_e, n_valid.reshape(1).astype(I32)


def _moe_grouped(xs, ws, tile_e, n_valid, w_gate, w_up, w_down, layer, tm):
    r, d = xs.shape
    f = w_gate.shape[-1]

    def body(te_ref, nv_ref, x_ref, w_ref, wg_ref, wu_ref, wd_ref, o_ref):
        @pl.when(pl.program_id(0) < nv_ref[0])
        def _():
            x = x_ref[...]
            a = _bdot(x, wg_ref[...])
            b = _bdot(x, wu_ref[...])
            act = a * _sigmoid(a) * b * w_ref[...]
            o_ref[...] = _bdot(act, wd_ref[...])

        @pl.when(pl.program_id(0) >= nv_ref[0])
        def _():
            o_ref[...] = jnp.zeros(o_ref.shape, F32)

    def xmap(i, te, nv):
        return (jnp.minimum(i, nv[0] - 1), 0)

    gs = pltpu.PrefetchScalarGridSpec(
        num_scalar_prefetch=2, grid=(r // tm,),
        in_specs=[pl.BlockSpec((tm, d), xmap), pl.BlockSpec((tm, 1), xmap),
                  pl.BlockSpec((None, None, d, f), lambda i, te, nv: (layer, te[i], 0, 0)),
                  pl.BlockSpec((None, None, d, f), lambda i, te, nv: (layer, te[i], 0, 0)),
                  pl.BlockSpec((None, None, f, d), lambda i, te, nv: (layer, te[i], 0, 0))],
        out_specs=pl.BlockSpec((tm, d), lambda i, te, nv: (i, 0)))
    return pl.pallas_call(body, grid_spec=gs, out_shape=jax.ShapeDtypeStruct((r, d), F32),
                          compiler_params=_cparams(("arbitrary",)), name="moe_grouped")(
        tile_e, n_valid, xs, ws, w_gate, w_up, w_down)


GDN_P_COLS = 4 * GDN_WIDTH + LANES


def _softplus(x):
    return jnp.maximum(x, 0.0) + jnp.log(1.0 + jnp.exp(-jnp.abs(x)))


def _gdn_gates(ba, al, dtb):
    return _sigmoid(ba), -jnp.exp(al) * _softplus(ba + dtb)


def _gdn_prompt(p, conv_w, a_log, dt_bias, out_norm):
    t = p.shape[0]
    c, nh, dk = GDN_CHUNK, GDN_HEADS, GDN_DK
    w3 = 3 * GDN_WIDTH

    def body(qkv_ref, z_ref, ba_ref, cw_ref, al_ref, dtb_ref, gn_ref, o_ref, sfin_ref, xbuf, s_ref):
        i = pl.program_id(0)

        @pl.when(i == 0)
        def _():
            xbuf[0:8, :] = jnp.zeros((8, w3), F32)
            s_ref[...] = jnp.zeros(s_ref.shape, F32)

        xbuf[8:8 + c, :] = qkv_ref[...]
        acc = cw_ref[0:1, :] * xbuf[5:5 + c, :]
        for j in range(1, CONV_W):
            acc = acc + cw_ref[j:j + 1, :] * xbuf[5 + j:5 + j + c, :]
        xbuf[0:8, :] = xbuf[c:c + 8, :]
        qkv = acc * _sigmoid(acc)

        def heads(off):
            return jnp.stack([qkv[:, off + h * dk: off + (h + 1) * dk] for h in range(nh)], axis=0)

        q, k, v = heads(0), heads(GDN_WIDTH), heads(2 * GDN_WIDTH)
        q = q * lax.rsqrt(jnp.sum(q * q, axis=-1, keepdims=True) + L2_EPS) * (dk ** -0.5)
        k = k * lax.rsqrt(jnp.sum(k * k, axis=-1, keepdims=True) + L2_EPS)
        beta_l, g_l = _gdn_gates(ba_ref[...], al_ref[...], dtb_ref[...])
        row = lax.broadcasted_iota(I32, (c, c), 0)
        col = lax.broadcasted_iota(I32, (c, c), 1)
        tril = (row >= col).astype(F32)
        gc_l = jnp.dot(tril, g_l, precision=HI, preferred_element_type=F32)
        beta = jnp.stack([beta_l[:, h:h + 1] for h in range(nh)], axis=0)
        gcol = jnp.stack([gc_l[:, nh + h:nh + h + 1] for h in range(nh)], axis=0)
        eye = (row == col)[None]
        grow = jnp.sum(jnp.where(eye, gcol, 0.0), axis=1, keepdims=True)
        glast = gcol[:, c - 1:c, :]
        causal = (row >= col)[None]
        strict = (row > col)[None]
        decay = jnp.exp(jnp.where(causal, gcol - grow, -jnp.inf))
        eg = jnp.exp(gcol)
        kb = k * beta
        kk = jnp.einsum("hid,hjd->hij", kb.astype(BF16), k.astype(BF16), preferred_element_type=F32)
        nmat = -jnp.where(strict, kk * decay, 0.0)
        tmat = jnp.where(eye, 1.0, 0.0) + nmat
        pw = nmat
        for _ in range(5):
            pw = jnp.einsum("hij,hjk->hik", pw, pw, precision=HI, preferred_element_type=F32)
            tmat = tmat + jnp.einsum("hij,hjk->hik", tmat, pw, precision=HI, preferred_element_type=F32)
        tb = tmat.astype(BF16)
        u = jnp.einsum("hij,hjd->hid", tb, (v * beta).astype(BF16), preferred_element_type=F32)
        w = jnp.einsum("hij,hjd->hid", tb, (kb * eg).astype(BF16), preferred_element_type=F32)
        attn = jnp.einsum("hid,hjd->hij", q.astype(BF16), k.astype(BF16), preferred_element_type=F32) * decay
        qg = q * eg
        kd = k * jnp.exp(glast - gcol)
        s = s_ref[...]
        sb = s.astype(BF16)
        v_new = u - jnp.einsum("hik,hkd->hid", w.astype(BF16), sb, preferred_element_type=F32)
        vb = v_new.astype(BF16)
        o = (jnp.einsum("hik,hkd->hid", qg.astype(BF16), sb, preferred_element_type=F32)
             + jnp.einsum("hij,hjd->hid", attn.astype(BF16), vb, preferred_element_type=F32))
        s_new = s * jnp.exp(glast) + jnp.einsum("hik,hid->hkd", kd.astype(BF16), vb, preferred_element_type=F32)
        s_ref[...] = s_new
        sfin_ref[...] = s_new
        z = z_ref[...]
        for h in range(nh):
            zh = z[:, h * dk:(h + 1) * dk]
            o_ref[:, h * dk:(h + 1) * dk] = _rms(o[h], gn_ref[...]) * (zh * _sigmoid(zh))

    return pl.pallas_call(
        body, grid=(t // c,),
        in_specs=[pl.BlockSpec((c, w3), lambda i: (i, 0)), pl.BlockSpec((c, GDN_WIDTH), lambda i: (i, 3)),
                  pl.BlockSpec((c, LANES), lambda i: (i, 4 * GDN_WIDTH // LANES)),
                  pl.BlockSpec((CONV_W, w3), lambda i: (0, 0)),
                  pl.BlockSpec((1, LANES), lambda i: (0, 0)), pl.BlockSpec((1, LANES), lambda i: (0, 0)),
                  pl.BlockSpec((1, GDN_DV), lambda i: (0, 0))],
        out_specs=[pl.BlockSpec((c, GDN_WIDTH), lambda i: (i, 0)),
                   pl.BlockSpec((nh, dk, GDN_DV), lambda i: (0, 0, 0))],
        out_shape=[jax.ShapeDtypeStruct((t, GDN_WIDTH), F32), jax.ShapeDtypeStruct((nh, dk, GDN_DV), F32)],
        scratch_shapes=[pltpu.VMEM((c + 8, w3), F32), pltpu.VMEM((nh, dk, GDN_DV), F32)],
        compiler_params=_cparams(("arbitrary",)), name="gdn_prompt")(p, p, p, conv_w, a_log, dt_bias, out_norm)


def _gdn_decode(p3, conv_state, state, conv_w, a_log, dt_bias, out_norm, layer):
    b = p3.shape[0]
    nh, dk = GDN_HEADS, GDN_DK
    w3 = 3 * GDN_WIDTH

    def body(qkv_ref, z_ref, ba_ref, cs_ref, st_ref, cw_ref, al_ref, dtb_ref, gn_ref, o_ref, ncs_ref, nst_ref):
        x = qkv_ref[...]
        cs = cs_ref[...]
        rows = [cs[:, j * w3:(j + 1) * w3] for j in range(CONV_W - 1)] + [x]
        acc = cw_ref[0:1, :] * rows[0]
        for j in range(1, CONV_W):
            acc = acc + cw_ref[j:j + 1, :] * rows[j]
        ncs_ref[...] = jnp.concatenate(rows[1:], axis=1)
        qkv = acc * _sigmoid(acc)

        def heads(off):
            return jnp.concatenate([qkv[:, off + h * dk: off + (h + 1) * dk] for h in range(nh)], axis=0)

        q, k, v = heads(0), heads(GDN_WIDTH), heads(2 * GDN_WIDTH)
        q = q * lax.rsqrt(jnp.sum(q * q, axis=-1, keepdims=True) + L2_EPS) * (dk ** -0.5)
        k = k * lax.rsqrt(jnp.sum(k * k, axis=-1, keepdims=True) + L2_EPS)
        qk = jnp.sum(q * k, axis=-1, keepdims=True)
        beta_l, g_l = _gdn_gates(ba_ref[...], al_ref[...], dtb_ref[...])
        kt = k.T
        qt = q.T
        z = z_ref[...]
        for h in range(nh):
            s = st_ref[h]
            bh = beta_l[:, h:h + 1]
            egh = jnp.exp(g_l[:, nh + h:nh + h + 1])
            kc = kt[:, h:h + 1]
            qc = qt[:, h:h + 1]
            ks = jnp.sum(kc * s, axis=0, keepdims=True)
            qs = jnp.sum(qc * s, axis=0, keepdims=True)
            v_new = v[h:h + 1, :] * bh - ks * (bh * egh)
            o = qs * egh + qk[h:h + 1, :] * v_new
            nst_ref[h] = s * egh + kc * v_new
            zh = z[:, h * dk:(h + 1) * dk]
            o_ref[:, h * dk:(h + 1) * dk] = _rms(o, gn_ref[...]) * (zh * _sigmoid(zh))

    return pl.pallas_call(
        body, grid=(b,),
        in_specs=[pl.BlockSpec((None, 1, w3), lambda i: (i, 0, 0)),
                  pl.BlockSpec((None, 1, GDN_WIDTH), lambda i: (i, 0, 3)),
                  pl.BlockSpec((None, 1, LANES), lambda i: (i, 0, 4 * GDN_WIDTH // LANES)),
                  pl.BlockSpec((None, None, 1, 3 * w3), lambda i: (i, layer, 0, 0)),
                  pl.BlockSpec((None, None, nh, dk, GDN_DV), lambda i: (i, layer, 0, 0, 0)),
                  pl.BlockSpec((CONV_W, w3), lambda i: (0, 0)),
                  pl.BlockSpec((1, LANES), lambda i: (0, 0)), pl.BlockSpec((1, LANES), lambda i: (0, 0)),
                  pl.BlockSpec((1, GDN_DV), lambda i: (0, 0))],
        out_specs=[pl.BlockSpec((None, 1, GDN_WIDTH), lambda i: (i, 0, 0)),
                   pl.BlockSpec((None, 1, 3 * w3), lambda i: (i, 0, 0)),
                   pl.BlockSpec((None, nh, dk, GDN_DV), lambda i: (i, 0, 0, 0))],
        out_shape=[jax.ShapeDtypeStruct((b, 1, GDN_WIDTH), F32), jax.ShapeDtypeStruct((b, 1, 3 * w3), F32),
                   jax.ShapeDtypeStruct((b, nh, dk, GDN_DV), F32)],
        compiler_params=_cparams(("arbitrary",)), name="gdn_decode")(
        p3, p3, p3, conv_state, state, conv_w, a_log, dt_bias, out_norm)


MLA_P_COLS = MLA_Q_LORA + 2 * LANES


def _rope_tables(pos):
    half = MLA_ROPE // 2
    inv = ROPE_THETA ** (-jnp.arange(half, dtype=F32) / half)
    ang = jnp.tile(pos.astype(F32)[:, None] * inv, (1, LANES // half))
    sign = jnp.where((jnp.arange(LANES) % MLA_ROPE) < half, -1.0, 1.0).astype(F32)
    return jnp.cos(ang), jnp.sin(ang) * sign


def _mla_weights(q_a_norm, w_uq, qk_norm, kv_norm, krope_norm, w_uk, w_uv):
    nh = MLA_HEADS
    w4 = w_uq.reshape(MLA_Q_LORA, nh, MLA_QK)
    w_uq_p = jnp.concatenate([w4[:, :, :MLA_NOPE].reshape(MLA_Q_LORA, nh * MLA_NOPE),
                              w4[:, :, MLA_NOPE:].reshape(MLA_Q_LORA, nh * MLA_ROPE)], axis=1).astype(BF16)
    g_nope = qk_norm[None, :MLA_NOPE]
    g_rope = jnp.tile(qk_norm[MLA_NOPE:], nh)[None]
    w_uk_t = jnp.transpose(w_uk, (1, 2, 0)).astype(BF16)
    w_uv_t = jnp.transpose(w_uv, (1, 0, 2)).astype(BF16)
    g_kr = jnp.pad(krope_norm, (0, LANES - MLA_ROPE))[None]
    return q_a_norm[None], w_uq_p, g_nope, g_rope, w_uk_t, kv_norm[None], g_kr, w_uv_t


def _rope_rotate(x, cos, sin_signed):
    lane = lax.broadcasted_iota(I32, x.shape, 1)
    half = MLA_ROPE // 2
    swapped = jnp.where(lane % MLA_ROPE < half, pltpu.roll(x, LANES - half, 1), pltpu.roll(x, half, 1))
    return x * cos + swapped * sin_signed


def _mla_prep(p, cos, sin, g_qa, w_uq, g_nope, g_rope, w_uk, g_kv, g_kr, tm, col_blk=0):
    m = p.shape[0]
    nh = MLA_HEADS

    def body(p_ref, cos_ref, sin_ref, gqa_ref, wuq_ref, gn_ref, gr_ref, wuk_ref, gkv_ref, gkr_ref, q_ref, kv_ref):
        cos, sin = cos_ref[...], sin_ref[...]
        cq = _rms(p_ref[:, 0:MLA_Q_LORA], gqa_ref[...])
        qf = _bdot(cq, wuq_ref[...])
        rope_all = qf[:, nh * MLA_NOPE:]
        r2 = rope_all * rope_all
        lane = lax.broadcasted_iota(I32, rope_all.shape, 1)
        inv_lane = jnp.zeros(rope_all.shape, F32)
        for h in range(nh):
            nope = qf[:, h * MLA_NOPE:(h + 1) * MLA_NOPE]
            in_h = lane // MLA_ROPE == h
            ss = jnp.sum(nope * nope, axis=-1, keepdims=True) + jnp.sum(jnp.where(in_h, r2, 0.0), axis=-1, keepdims=True)
            inv = lax.rsqrt(ss / MLA_QK + RMS_EPS)
            inv_lane = jnp.where(in_h, inv, inv_lane)
            q_ref[h, :, 0:MLA_KV_LORA] = _bdot(nope * inv * gn_ref[...], wuk_ref[h])
        rot = _rope_rotate(rope_all * inv_lane * gr_ref[...], cos, sin)
        for h in range(nh):
            q_ref[h, :, MLA_KV_LORA:MLA_LAT] = rot[:, h * MLA_ROPE:(h + 1) * MLA_ROPE]
        kv_ref[:, 0:MLA_KV_LORA] = _rms(p_ref[:, MLA_Q_LORA:MLA_Q_LORA + MLA_KV_LORA], gkv_ref[...])
        kr = p_ref[:, MLA_Q_LORA + MLA_KV_LORA:MLA_P_COLS]
        kv_ref[:, MLA_KV_LORA:MLA_LAT] = _rope_rotate(_rms(kr, gkr_ref[...], MLA_ROPE), cos, sin)[:, 0:MLA_ROPE]

    full = lambda a: pl.BlockSpec(a.shape, lambda i: (0,) * a.ndim)
    return pl.pallas_call(
        body, grid=(m // tm,),
        in_specs=[pl.BlockSpec((tm, MLA_P_COLS), lambda i: (i, col_blk)), pl.BlockSpec((tm, LANES), lambda i: (i, 0)),
                  pl.BlockSpec((tm, LANES), lambda i: (i, 0)), full(g_qa), full(w_uq), full(g_nope), full(g_rope),
                  full(w_uk), full(g_kv), full(g_kr)],
        out_specs=[pl.BlockSpec((nh, tm, MLA_LAT), lambda i: (0, i, 0)), pl.BlockSpec((tm, MLA_LAT), lambda i: (i, 0))],
        out_shape=[jax.ShapeDtypeStruct((nh, m, MLA_LAT), F32), jax.ShapeDtypeStruct((m, MLA_LAT), F32)],
        compiler_params=_cparams(("arbitrary",)), name="mla_prep")(
        p, cos, sin, g_qa, w_uq, g_nope, g_rope, w_uk, g_kv, g_kr)


def _mla_flash(q, kv, tq, tk):
    nh, t, _ = q.shape
    scale = MLA_QK ** -0.5

    def body(q_ref, kv_ref, o_ref, m_ref, l_ref, acc_ref):
        i, j = pl.program_id(0), pl.program_id(1)

        @pl.when(j == 0)
        def _():
            m_ref[...] = jnp.full(m_ref.shape, -jnp.inf, F32)
            l_ref[...] = jnp.zeros(l_ref.shape, F32)
            acc_ref[...] = jnp.zeros(acc_ref.shape, F32)

        @pl.when(j * tk <= i * tq + tq - 1)
        def _():
            kb = kv_ref[...].astype(BF16)
            s = _bdot_nt(q_ref[...].reshape(nh * tq, MLA_LAT), kb) * scale
            qpos = i * tq + lax.broadcasted_iota(I32, (nh, tq, tk), 1).reshape(nh * tq, tk)
            kpos = j * tk + lax.broadcasted_iota(I32, (nh * tq, tk), 1)
            s = jnp.where(kpos <= qpos, s, -jnp.inf)
            m_new = jnp.maximum(m_ref[...], jnp.max(s, axis=-1, keepdims=True))
            alpha = jnp.exp(m_ref[...] - m_new)
            pr = jnp.exp(s - m_new)
            l_ref[...] = l_ref[...] * alpha + jnp.sum(pr, axis=-1, keepdims=True)
            acc_ref[...] = acc_ref[...] * alpha + jnp.dot(pr.astype(BF16), kb[:, 0:MLA_KV_LORA], preferred_element_type=F32)
            m_ref[...] = m_new

        @pl.when(j == pl.num_programs(1) - 1)
        def _():
            o_ref[...] = (acc_ref[...] / l_ref[...]).reshape(nh, tq, MLA_KV_LORA)

    return pl.pallas_call(
        body, grid=(t // tq, t // tk),
        in_specs=[pl.BlockSpec((nh, tq, MLA_LAT), lambda i, j: (0, i, 0)),
                  pl.BlockSpec((tk, MLA_LAT), lambda i, j: (jnp.minimum(j, (i * tq + tq - 1) // tk), 0))],
        out_specs=pl.BlockSpec((nh, tq, MLA_KV_LORA), lambda i, j: (0, i, 0)),
        out_shape=jax.ShapeDtypeStruct((nh, t, MLA_KV_LORA), F32),
        scratch_shapes=[pltpu.VMEM((nh * tq, 1), F32), pltpu.VMEM((nh * tq, 1), F32),
                        pltpu.VMEM((nh * tq, MLA_KV_LORA), F32)],
        compiler_params=_cparams(("arbitrary", "arbitrary")), name="mla_flash")(q, kv)


PAGES_PER_STEP = 16


def _gather_pipeline(step, n_steps, copies_of, n_copies):
    slot = step % 2

    @pl.when(step == 0)
    def _():
        for k in range(n_copies):
            copies_of(0, 0, k).start()

    @pl.when(step + 1 < n_steps)
    def _():
        for k in range(n_copies):
            copies_of(step + 1, 1 - slot, k).start()

    for k in range(n_copies):
        copies_of(step, slot, k).wait()
    return slot


def _mla_decode(q8, knew, cache, page_table, layer):
    b, n_pages = page_table.shape
    pp = min(PAGES_PER_STEP, n_pages)
    nc = n_pages // pp
    scale = MLA_QK ** -0.5

    def body(pt_ref, q_ref, kn_ref, cache_ref, o_ref, buf, sem, m_ref, l_ref, acc_ref):
        c = pl.program_id(1)
        step = pl.program_id(0) * nc + c

        def page_copy(st, sl, jj):
            page = pt_ref[st // nc, (st % nc) * pp + jj]
            return pltpu.make_async_copy(cache_ref.at[page, layer], buf.at[sl, jj], sem.at[sl])

        slot = _gather_pipeline(step, b * nc, page_copy, pp)

        @pl.when(c == 0)
        def _():
            m_ref[...] = jnp.full(m_ref.shape, -jnp.inf, F32)
            l_ref[...] = jnp.zeros(l_ref.shape, F32)
            acc_ref[...] = jnp.zeros(acc_ref.shape, F32)

        qb = q_ref[...].astype(BF16)
        kbs = [buf[slot, jj].astype(BF16) for jj in range(pp)]
        s = jnp.concatenate([_bdot_nt(qb, kb) for kb in kbs], axis=1) * scale
        m_new = jnp.maximum(m_ref[...], jnp.max(s, axis=-1, keepdims=True))
        alpha = jnp.exp(m_ref[...] - m_new)
        pr = jnp.exp(s - m_new)
        l_ref[...] = l_ref[...] * alpha + jnp.sum(pr, axis=-1, keepdims=True)
        pv = acc_ref[...] * alpha
        for jj, kb in enumerate(kbs):
            pv = pv + jnp.dot(pr[:, jj * PAGE_SIZE:(jj + 1) * PAGE_SIZE].astype(BF16), kb[:, 0:MLA_KV_LORA],
                              preferred_element_type=F32)
        acc_ref[...] = pv
        m_ref[...] = m_new

        @pl.when(c == nc - 1)
        def _():
            kn = kn_ref[...]
            sn = jnp.sum(q_ref[...] * kn, axis=-1, keepdims=True) * scale
            m2 = jnp.maximum(m_ref[...], sn)
            a2 = jnp.exp(m_ref[...] - m2)
            pn = jnp.exp(sn - m2)
            l2 = l_ref[...] * a2 + pn
            o_ref[...] = (acc_ref[...] * a2 + pn * kn[:, 0:MLA_KV_LORA]) / l2

    gs = pltpu.PrefetchScalarGridSpec(
        num_scalar_prefetch=1, grid=(b, nc),
        in_specs=[pl.BlockSpec((None, 8, MLA_LAT), lambda i, c, pt: (i, 0, 0)),
                  pl.BlockSpec((None, 1, MLA_LAT), lambda i, c, pt: (i, 0, 0)), pl.BlockSpec(memory_space=pl.ANY)],
        out_specs=pl.BlockSpec((None, 8, MLA_KV_LORA), lambda i, c, pt: (i, 0, 0)),
        scratch_shapes=[pltpu.VMEM((2, pp, PAGE_SIZE, MLA_LAT), F32), pltpu.SemaphoreType.DMA((2,)),
                        pltpu.VMEM((8, 1), F32), pltpu.VMEM((8, 1), F32), pltpu.VMEM((8, MLA_KV_LORA), F32)])
    return pl.pallas_call(body, grid_spec=gs, out_shape=jax.ShapeDtypeStruct((b, 8, MLA_KV_LORA), F32),
                          compiler_params=_cparams(("arbitrary", "arbitrary")), name="mla_decode")(
        page_table, q8, knew, cache)


NSA_P_COLS = NSA_WIDTH + 7 * LANES
KV_W = 2 * NSA_HEAD_DIM


def _nsa_prep(p, g_q, g_ksel, g_kwin, tm, col_blk=0):
    m = p.shape[0]
    dh = NSA_HEAD_DIM

    def body(p_ref, gq_ref, gs_ref, gw_ref, q_ref, c_ref, s_ref, w_ref, g_ref):
        for h in range(NSA_HEADS):
            q_ref[:, h * dh:(h + 1) * dh] = _rms(p_ref[:, h * dh:(h + 1) * dh], gq_ref[...])
        o = NSA_WIDTH
        c_ref[...] = p_ref[:, o:o + 2 * dh]
        s_ref[:, 0:dh] = _rms(p_ref[:, o + 2 * dh:o + 3 * dh], gs_ref[...])
        s_ref[:, dh:2 * dh] = p_ref[:, o + 3 * dh:o + 4 * dh]
        w_ref[:, 0:dh] = _rms(p_ref[:, o + 4 * dh:o + 5 * dh], gw_ref[...])
        w_ref[:, dh:2 * dh] = p_ref[:, o + 5 * dh:o + 6 * dh]
        g_ref[...] = _sigmoid(p_ref[:, o + 6 * dh:o + 7 * dh])

    row = lambda w: pl.BlockSpec((tm, w), lambda i: (i, 0))
    vec = pl.BlockSpec((1, dh), lambda i: (0, 0))
    return pl.pallas_call(
        body, grid=(m // tm,), in_specs=[pl.BlockSpec((tm, NSA_P_COLS), lambda i: (i, col_blk)), vec, vec, vec],
        out_specs=[row(NSA_WIDTH), row(KV_W), row(KV_W), row(KV_W), row(LANES)],
        out_shape=[jax.ShapeDtypeStruct((m, w), F32) for w in (NSA_WIDTH, KV_W, KV_W, KV_W, LANES)],
        compiler_params=_cparams(("arbitrary",)), name="nsa_prep")(p, g_q, g_ksel, g_kwin)


def _gelu_tanh(x):
    return 0.5 * x * (1.0 + jnp.tanh(math.sqrt(2.0 / math.pi) * (x + 0.044715 * x * x * x)))


def _compress_tail(acc_k, acc_v, w2k_ref, w2v_ref, gk_ref):
    kc = _rms(_bdot(_gelu_tanh(acc_k), w2k_ref[...]), gk_ref[...])
    vc = _bdot(_gelu_tanh(acc_v), w2v_ref[...])
    return kc, vc


def _nsa_compress(xb, pe, w1k, w1v, w2k, w2v, g_k):
    nb = xb.shape[0]
    bm = min(nb, 256)
    dh = NSA_HEAD_DIM

    def body(x_ref, pe_ref, w1k_ref, w1v_ref, w2k_ref, w2v_ref, gk_ref, kc_ref, vc_ref):
        x = x_ref[...] + pe_ref[...]
        xk = jnp.concatenate([x[:, r * KV_W:r * KV_W + dh] for r in range(CMP_BLOCK)], axis=1)
        xv = jnp.concatenate([x[:, r * KV_W + dh:(r + 1) * KV_W] for r in range(CMP_BLOCK)], axis=1)
        kc, vc = _compress_tail(_bdot(xk, w1k_ref[...]), _bdot(xv, w1v_ref[...]), w2k_ref, w2v_ref, gk_ref)
        kc_ref[...] = kc
        vc_ref[...] = vc

    full = lambda a: pl.BlockSpec(a.shape, lambda i: (0,) * a.ndim)
    return pl.pallas_call(
        body, grid=(nb // bm,),
        in_specs=[pl.BlockSpec((bm, CMP_BLOCK * KV_W), lambda i: (i, 0)), full(pe), full(w1k), full(w1v), full(w2k),
                  full(w2v), full(g_k)],
        out_specs=[pl.BlockSpec((bm, dh), lambda i: (i, 0)), pl.BlockSpec((bm, dh), lambda i: (i, 0))],
        out_shape=[jax.ShapeDtypeStruct((nb, dh), F32), jax.ShapeDtypeStruct((nb, dh), F32)],
        compiler_params=_cparams(("arbitrary",)), name="nsa_compress")(xb, pe, w1k, w1v, w2k, w2v, g_k)


CMP_PAGES = 64
BLOCKS_PER_PAGE = PAGE_SIZE // CMP_BLOCK


def _nsa_compress_paged(cache, page_table, layer, pe2, w1k, w1v, w2k, w2v, g_k):
    b, n_pages = page_table.shape
    pp = min(CMP_PAGES, n_pages)
    nc = n_pages // pp
    nblk = pp * BLOCKS_PER_PAGE
    dh = NSA_HEAD_DIM
    n_steps = b * nc

    def body(pt_ref, cache_ref, pe_ref, w1k_ref, w1v_ref, w2k_ref, w2v_ref, gk_ref, kc_ref, vc_ref, buf, sem):
        step = pl.program_id(0) * nc + pl.program_id(1)

        def page_copy(st, sl, k):
            jj, half = k // 2, k % 2
            page = pt_ref[st // nc, (st % nc) * pp + jj]
            return pltpu.make_async_copy(cache_ref.at[page, layer, :, pl.ds(half * dh, dh)],
                                         buf.at[sl, half, pl.ds(jj * PAGE_SIZE, PAGE_SIZE)], sem.at[sl])

        slot = _gather_pipeline(step, n_steps, page_copy, 2 * pp)

        def rows(half, r):
            return buf[slot, half, pl.ds(r, nblk, stride=CMP_BLOCK), :] + pe_ref[r:r + 1, half * dh:(half + 1) * dh]

        acc_k = jnp.zeros((nblk, CMP_HIDDEN), F32)
        acc_v = jnp.zeros((nblk, CMP_HIDDEN), F32)
        for r in range(0, CMP_BLOCK, 2):
            xk = jnp.concatenate([rows(0, r), rows(0, r + 1)], axis=1)
            xv = jnp.concatenate([rows(1, r), rows(1, r + 1)], axis=1)
            acc_k = acc_k + _bdot(xk, w1k_ref[r * dh:(r + 2) * dh, :])
            acc_v = acc_v + _bdot(xv, w1v_ref[r * dh:(r + 2) * dh, :])
        kc, vc = _compress_tail(acc_k, acc_v, w2k_ref, w2v_ref, gk_ref)
        kc_ref[...] = kc
        vc_ref[...] = vc

    full = lambda a: pl.BlockSpec(a.shape, lambda i, c, pt: (0,) * a.ndim)
    gs = pltpu.PrefetchScalarGridSpec(
        num_scalar_prefetch=1, grid=(b, nc),
        in_specs=[pl.BlockSpec(memory_space=pl.ANY), full(pe2), full(w1k), full(w1v), full(w2k), full(w2v), full(g_k)],
        out_specs=[pl.BlockSpec((None, nblk, dh), lambda i, c, pt: (i, c, 0)),
                   pl.BlockSpec((None, nblk, dh), lambda i, c, pt: (i, c, 0))],
        scratch_shapes=[pltpu.VMEM((2, 2, pp * PAGE_SIZE, dh), F32), pltpu.SemaphoreType.DMA((2,))])
    shp = jax.ShapeDtypeStruct((b, n_pages * BLOCKS_PER_PAGE, dh), F32)
    return pl.pallas_call(body, grid_spec=gs, out_shape=[shp, shp],
                          compiler_params=_cparams(("arbitrary", "arbitrary")), name="nsa_compress_paged")(
        page_table, cache, pe2, w1k, w1v, w2k, w2v, g_k)


def _masked_softmax(l, mask):
    l = jnp.where(mask, l, -jnp.inf)
    mx = jnp.max(l, axis=-1, keepdims=True)
    mx = jnp.where(mx == -jnp.inf, 0.0, mx)
    p = jnp.where(mask, jnp.exp(l - mx), 0.0)
    return p / jnp.maximum(jnp.sum(p, axis=-1, keepdims=True), 1e-30)


def _select_blocks(imp, n_lanes, n_sel):
    lane = lax.broadcasted_iota(I32, imp.shape, 1)
    picked = jnp.zeros(imp.shape, F32)
    ids = []
    for _ in range(n_sel):
        mx = jnp.max(imp, axis=-1, keepdims=True)
        idx = jnp.min(jnp.where(imp == mx, lane, n_lanes), axis=-1, keepdims=True)
        hit = lane == idx
        picked = jnp.where(hit, 1.0, picked)
        imp = jnp.where(hit, -2.0, imp)
        ids.append(idx)
    return picked, ids


def _stack_heads(x, nh, dh):
    return jnp.concatenate([x[:, h * dh:(h + 1) * dh] for h in range(nh)], axis=0)


def _nsa_cmp_select(qn, kcmp, vcmp, pair, rel_bias):
    t = qn.shape[0]
    nc, ns = pair.shape
    nh, dh, tq = NSA_HEADS, NSA_HEAD_DIM, Q_TILE
    n_sel = min(N_SEL, ns)
    scale = dh ** -0.5

    def body(rb_ref, q_ref, kc_ref, vc_ref, pair_ref, o_ref, sel_ref):
        t0 = pl.program_id(0) * tq
        q4 = _stack_heads(q_ref[...], nh, dh)
        lc = (_bdot_nt(q4, kc_ref[...]) * scale).reshape(nh, tq, nc)
        tpos = t0 + lax.broadcasted_iota(I32, (tq, nc), 0)
        dist = tpos - (lax.broadcasted_iota(I32, (tq, nc), 1) * CMP_BLOCK + CMP_BLOCK - 1)
        lc = lc + jnp.stack([_t5_bias(dist, rb_ref, h) for h in range(nh)], axis=0)
        pc = _masked_softmax(lc, (dist >= 0)[None])
        oc = _bdot(pc.reshape(nh * tq, nc), vc_ref[...])
        for h in range(nh):
            o_ref[:, h * dh:(h + 1) * dh] = oc[h * tq:(h + 1) * tq, :]
        imp = jnp.dot(jnp.sum(pc, axis=0), pair_ref[...], precision=HI, preferred_element_type=F32)
        blk = lax.broadcasted_iota(I32, (tq, ns), 1)
        tq_pos = t0 + lax.broadcasted_iota(I32, (tq, ns), 0)
        cur = tq_pos // SEL_BLOCK
        forced = (blk == 0) | (blk == cur) | (blk == cur - 1)
        valid = blk * SEL_BLOCK <= tq_pos
        imp = jnp.where(forced, jnp.inf, jnp.where(valid, imp, -1.0))
        sel_ref[...] = _select_blocks(imp, ns, n_sel)[0]

    gs = pltpu.PrefetchScalarGridSpec(
        num_scalar_prefetch=0, grid=(t // tq,),
        in_specs=[pl.BlockSpec(memory_space=pltpu.SMEM),
                  pl.BlockSpec((tq, NSA_WIDTH), lambda i: (i, 0)), pl.BlockSpec((nc, dh), lambda i: (0, 0)),
                  pl.BlockSpec((nc, dh), lambda i: (0, 0)), pl.BlockSpec((nc, ns), lambda i: (0, 0))],
        out_specs=[pl.BlockSpec((tq, NSA_WIDTH), lambda i: (i, 0)), pl.BlockSpec((tq, ns), lambda i: (i, 0))])
    return pl.pallas_call(body, grid_spec=gs,
                          out_shape=[jax.ShapeDtypeStruct((t, NSA_WIDTH), F32), jax.ShapeDtypeStruct((t, ns), F32)],
                          compiler_params=_cparams(("arbitrary",)), name="nsa_cmp_select")(rel_bias, qn, kcmp, vcmp, pair)


N_BIAS_TILES = 4


def _nsa_flash(qn, gates, o_cmp, selmask, sel_kv, win_kv, rel_bias):
    t = qn.shape[0]
    ns = selmask.shape[1]
    nh, dh, tq, tk = NSA_HEADS, NSA_HEAD_DIM, Q_TILE, K_TILE
    nk = t // tk
    scale = dh ** -0.5
    spb = tk // SEL_BLOCK

    def last_tile(i):
        return (i * tq + tq - 1) // tk

    def body(rb_ref, q_ref, g_ref, oc_ref, sm_ref, sk_ref, wk_ref, o_ref, bt_ref, ms, ls, accs, mw, lw, accw):
        i, j = pl.program_id(0), pl.program_id(1)
        t0, k0 = i * tq, j * tk

        @pl.when((i == 0) & (j == 0))
        def _():
            rel = lax.broadcasted_iota(I32, (tq, tk), 0) - lax.broadcasted_iota(I32, (tq, tk), 1)
            for d in range(N_BIAS_TILES):
                for h in range(nh):
                    bt_ref[d, h] = _t5_bias(rel + d * tq, rb_ref, h)

        @pl.when(j == 0)
        def _():
            for m_ref, l_ref, a_ref in ((ms, ls, accs), (mw, lw, accw)):
                m_ref[...] = jnp.full(m_ref.shape, -jnp.inf, F32)
                l_ref[...] = jnp.zeros(l_ref.shape, F32)
                a_ref[...] = jnp.zeros(a_ref.shape, F32)

        def update(m_ref, l_ref, a_ref, s, mask, vals):
            s = jnp.where(mask, s, -jnp.inf)
            m_old = m_ref[...]
            m_new = jnp.maximum(m_old, jnp.max(s, axis=-1, keepdims=True))
            m_safe = jnp.where(m_new == -jnp.inf, 0.0, m_new)
            alpha = jnp.exp(m_old - m_safe)
            pr = jnp.exp(s - m_safe)
            l_ref[...] = l_ref[...] * alpha + jnp.sum(pr, axis=-1, keepdims=True)
            a_ref[...] = a_ref[...] * alpha + _bdot(pr, vals)
            m_ref[...] = m_new

        @pl.when(j <= last_tile(i))
        def _():
            q4 = _stack_heads(q_ref[...], nh, dh)
            bias = bt_ref[jnp.minimum((t0 - k0) // tq, N_BIAS_TILES - 1)].reshape(nh * tq, tk)
            dist = (t0 - k0) + lax.broadcasted_iota(I32, (nh, tq, tk), 1) - lax.broadcasted_iota(I32, (nh, tq, tk), 2)
            dist = dist.reshape(nh * tq, tk)
            expand = (lax.broadcasted_iota(I32, (ns, tk), 0) == j * spb + lax.broadcasted_iota(I32, (ns, tk), 1) // SEL_BLOCK)
            chosen = _bdot(sm_ref[...], expand.astype(F32)) > 0.5
            chosen = jnp.broadcast_to(chosen[None], (nh, tq, tk)).reshape(nh * tq, tk)
            skv = sk_ref[...]
            s = _bdot_nt(q4, skv[:, 0:dh]) * scale + bias
            update(ms, ls, accs, s, chosen & (dist >= 0), skv[:, dh:KV_W])

            @pl.when(k0 + tk - 1 >= t0 - NSA_WINDOW)
            def _():
                wkv = wk_ref[...]
                sw = _bdot_nt(q4, wkv[:, 0:dh]) * scale + bias
                update(mw, lw, accw, sw, (dist >= 0) & (dist <= NSA_WINDOW), wkv[:, dh:KV_W])

        @pl.when(j == last_tile(i))
        def _():
            o_s = accs[...] / jnp.maximum(ls[...], 1e-30)
            o_w = accw[...] / jnp.maximum(lw[...], 1e-30)
            g = g_ref[...]
            for h in range(nh):
                rows = slice(h * tq, (h + 1) * tq)
                o_ref[:, h * dh:(h + 1) * dh] = (g[:, 3 * h:3 * h + 1] * oc_ref[:, h * dh:(h + 1) * dh]
                                                 + g[:, 3 * h + 1:3 * h + 2] * o_s[rows]
                                                 + g[:, 3 * h + 2:3 * h + 3] * o_w[rows])

    qrow = lambda w: pl.BlockSpec((tq, w), lambda i, j: (i, 0))
    krow = pl.BlockSpec((tk, KV_W), lambda i, j: (jnp.minimum(j, last_tile(i)), 0))
    gs = pltpu.PrefetchScalarGridSpec(
        num_scalar_prefetch=0, grid=(t // tq, nk),
        in_specs=[pl.BlockSpec(memory_space=pltpu.SMEM), qrow(NSA_WIDTH), qrow(LANES), qrow(NSA_WIDTH), qrow(ns), krow, krow],
        out_specs=qrow(NSA_WIDTH),
        scratch_shapes=[pltpu.VMEM((N_BIAS_TILES, nh, tq, tk), F32)]
        + [pltpu.VMEM((nh * tq, 1), F32), pltpu.VMEM((nh * tq, 1), F32), pltpu.VMEM((nh * tq, dh), F32)] * 2)
    return pl.pallas_call(body, grid_spec=gs, out_shape=jax.ShapeDtypeStruct((t, NSA_WIDTH), F32),
                          compiler_params=_cparams(("arbitrary", "arbitrary")), name="nsa_flash")(
        rel_bias, qn, gates, o_cmp, selmask, sel_kv, win_kv)


def _head_bias_rows(dist, rb_ref):
    n = dist.shape[1]
    return jnp.concatenate([_t5_bias(dist, rb_ref, h) for h in range(NSA_HEADS)]
                           + [jnp.zeros((8 - NSA_HEADS, n), F32)], axis=0)


def _nsa_cmp_select_decode(q8, kcmp, vcmp, pair, rel_bias, past):
    b, ncp, dh = kcmp.shape
    nsp = pair.shape[1]
    ns = (past + 1 + SEL_BLOCK - 1) // SEL_BLOCK
    ns_l = -(-ns // LANES) * LANES
    n_sel = min(N_SEL, ns)
    scale = dh ** -0.5
    assert past % PAGE_SIZE == 0 and ncp == past // CMP_BLOCK and nsp == past // SEL_BLOCK and ns_l > nsp

    def body(rb_ref, q_ref, kc_ref, vc_ref, pair_ref, o_ref, id_ref):
        lc = _bdot_nt(q_ref[...], kc_ref[...]) * scale
        dist = past - (lax.broadcasted_iota(I32, (1, ncp), 1) * CMP_BLOCK + CMP_BLOCK - 1)
        pc = _masked_softmax(lc + _head_bias_rows(dist, rb_ref), jnp.broadcast_to(dist >= 0, (8, ncp)))
        o_ref[...] = _bdot(pc, vc_ref[...])
        imp = jnp.dot(jnp.sum(pc[0:NSA_HEADS], axis=0, keepdims=True), pair_ref[...], precision=HI,
                      preferred_element_type=F32)
        imp = jnp.concatenate([imp, jnp.zeros((1, ns_l - nsp), F32)], axis=1)
        blk = lax.broadcasted_iota(I32, (1, ns_l), 1)
        cur = past // SEL_BLOCK
        forced = (blk == 0) | (blk == cur) | (blk == cur - 1)
        valid = blk * SEL_BLOCK <= past
        imp = jnp.where(blk < ns, jnp.where(forced, jnp.inf, jnp.where(valid, imp, -1.0)), -2.0)
        ids = _select_blocks(imp, ns_l, n_sel)[1]
        lane = lax.broadcasted_iota(I32, (1, LANES), 1)
        out = jnp.zeros((1, LANES), I32)
        for r, idx in enumerate(ids):
            out = jnp.where(lane == r, idx, out)
        id_ref[...] = out

    gs = pltpu.PrefetchScalarGridSpec(
        num_scalar_prefetch=0, grid=(b,),
        in_specs=[pl.BlockSpec(memory_space=pltpu.SMEM), pl.BlockSpec((None, 8, dh), lambda i: (i, 0, 0)),
                  pl.BlockSpec((None, ncp, dh), lambda i: (i, 0, 0)), pl.BlockSpec((None, ncp, dh), lambda i: (i, 0, 0)),
                  pl.BlockSpec((ncp, nsp), lambda i: (0, 0))],
        out_specs=[pl.BlockSpec((None, 8, dh), lambda i: (i, 0, 0)), pl.BlockSpec((None, 1, LANES), lambda i: (i, 0, 0))])
    return pl.pallas_call(body, grid_spec=gs,
                          out_shape=[jax.ShapeDtypeStruct((b, 8, dh), F32), jax.ShapeDtypeStruct((b, 1, LANES), I32)],
                          compiler_params=_cparams(("arbitrary",)), name="nsa_cmp_select_decode")(
        rel_bias, q8, kcmp, vcmp, pair)


def _nsa_sel_win_decode(q8, gates, o_cmp, new_sel, new_win, sel_ids, page_table, cache_sel, win_state, rel_bias, layer, past):
    b, n_pages = page_table.shape
    dh = NSA_HEAD_DIM
    n_sel = min(N_SEL, (past + 1 + SEL_BLOCK - 1) // SEL_BLOCK)
    n_cached = past // SEL_BLOCK
    halves = PAGE_SIZE // SEL_BLOCK
    wr = win_state.shape[2]
    keep = min(NSA_WINDOW, wr + 1)
    scale = dh ** -0.5
    assert wr == min(NSA_WINDOW, past) and keep == wr

    def body(ids_ref, pt_ref, rb_ref, q_ref, g_ref, oc_ref, ns_ref, nw_ref, ws_ref, cache_ref, o_ref, nwin_ref, buf, sem):
        i = pl.program_id(0)

        def block_copy(st, sl, r):
            bid = jnp.minimum(ids_ref[st, r], n_cached - 1)
            return pltpu.make_async_copy(
                cache_ref.at[pt_ref[st, bid // halves], layer, pl.ds((bid % halves) * SEL_BLOCK, SEL_BLOCK)],
                buf.at[sl, r], sem.at[sl])

        slot = _gather_pipeline(i, b, block_copy, n_sel)
        q = q_ref[...]
        zero = jnp.zeros((1, 1), I32)

        def attend(parts, new_row, new_penalty):
            sn = jnp.sum(q * new_row[:, 0:dh], axis=-1, keepdims=True) * scale + _head_bias_rows(zero, rb_ref)
            sn = sn + new_penalty
            masked = [jnp.where(v, s, -jnp.inf) for s, v, _ in parts]
            mx = sn
            for s in masked:
                mx = jnp.maximum(mx, jnp.max(s, axis=-1, keepdims=True))
            mx = jnp.where(mx == -jnp.inf, 0.0, mx)
            pn = jnp.exp(sn - mx)
            den = pn
            num = pn * new_row[:, dh:KV_W]
            for s, (_, _, vals) in zip(masked, parts):
                pr = jnp.exp(s - mx)
                den = den + jnp.sum(pr, axis=-1, keepdims=True)
                num = num + _bdot(pr, vals)
            return num / jnp.maximum(den, 1e-30)

        lane_b = lax.broadcasted_iota(I32, (1, SEL_BLOCK), 1)
        parts = []
        n_cur = 0
        for r in range(n_sel):
            bid = ids_ref[i, r]
            kv = buf[slot, r]
            dist = past - (bid * SEL_BLOCK + lane_b) - jnp.where(bid < n_cached, 0, 2 * (past + SEL_BLOCK))
            s = _bdot_nt(q, kv[:, 0:dh]) * scale + _head_bias_rows(dist, rb_ref)
            parts.append((s, dist >= 0, kv[:, dh:KV_W]))
            n_cur = n_cur + jnp.where(bid == past // SEL_BLOCK, 1, 0)
        o_s = attend(parts, ns_ref[...], jnp.where(n_cur > 0, 0.0, -jnp.inf))

        wkv = ws_ref[...]
        dist_w = wr - lax.broadcasted_iota(I32, (1, wr), 1)
        sw = _bdot_nt(q, wkv[:, 0:dh]) * scale + _head_bias_rows(dist_w, rb_ref)
        o_w = attend([(sw, (dist_w >= 0) & (dist_w <= NSA_WINDOW), wkv[:, dh:KV_W])], nw_ref[...], 0.0)
        nwin_ref[0:keep - 1, :] = ws_ref[wr + 1 - keep:wr, :]
        nwin_ref[keep - 1:keep, :] = nw_ref[...]

        g = jnp.broadcast_to(g_ref[...], (8, LANES))
        lane = lax.broadcasted_iota(I32, (8, LANES), 1)
        row = lax.broadcasted_iota(I32, (8, LANES), 0)
        gcol = [jnp.sum(jnp.where(lane == 3 * row + k, g, 0.0), axis=-1, keepdims=True) for k in range(3)]
        o_ref[...] = gcol[0] * oc_ref[...] + gcol[1] * o_s + gcol[2] * o_w

    per = lambda r, w: pl.BlockSpec((None, r, w), lambda i, ids, pt: (i, 0, 0))
    gs = pltpu.PrefetchScalarGridSpec(
        num_scalar_prefetch=2, grid=(b,),
        in_specs=[pl.BlockSpec(memory_space=pltpu.SMEM), per(8, dh), per(1, LANES), per(8, dh), per(1, KV_W), per(1, KV_W),
                  pl.BlockSpec((None, None, wr, KV_W), lambda i, ids, pt: (i, layer, 0, 0)),
                  pl.BlockSpec(memory_space=pl.ANY)],
        out_specs=[per(8, dh), per(keep, KV_W)],
        scratch_shapes=[pltpu.VMEM((2, n_sel, SEL_BLOCK, KV_W), F32), pltpu.SemaphoreType.DMA((2,))])
    return pl.pallas_call(body, grid_spec=gs,
                          out_shape=[jax.ShapeDtypeStruct((b, 8, dh), F32), jax.ShapeDtypeStruct((b, keep, KV_W), F32)],
                          compiler_params=_cparams(("arbitrary",)), name="nsa_sel_win_decode")(
        sel_ids, page_table, rel_bias, q8, gates, o_cmp, new_sel, new_win, win_state, cache_sel)


def _pair_matrix(nc, ns):
    r = SEL_BLOCK // CMP_BLOCK
    return (jnp.arange(nc)[:, None] // r == jnp.arange(ns)[None, :]).astype(F32)


def _nsa_weights(q_norm, k_norm, pe, w1, w2):
    pe2 = jnp.concatenate([pe[0], pe[1]], axis=1)
    return (q_norm[None], k_norm[0][None], k_norm[1][None], k_norm[2][None], pe2,
            w1[0].astype(BF16), w1[1].astype(BF16), w2[0].astype(BF16), w2[1].astype(BF16))


def _nsa_prompt(p, nw, rel_bias, col_blk=0):
    g_q, g_kc, g_ks, g_kw, pe2, w1k, w1v, w2k, w2v = nw
    t = p.shape[0]
    qn, cmp_rows, sel_rows, win_rows, gates = _nsa_prep(p, g_q, g_ks, g_kw, min(t, 512), col_blk)
    nc, ns = t // CMP_BLOCK, t // SEL_BLOCK
    kcmp, vcmp = _nsa_compress(cmp_rows.reshape(nc, CMP_BLOCK * KV_W), pe2.reshape(1, CMP_BLOCK * KV_W),
                               w1k, w1v, w2k, w2v, g_kc)
    o_cmp, selmask = _nsa_cmp_select(qn, kcmp, vcmp, _pair_matrix(nc, ns), rel_bias)
    o = _nsa_flash(qn, gates, o_cmp, selmask, sel_rows, win_rows, rel_bias)
    return o, cmp_rows, sel_rows, win_rows


def _nsa_decode(p, cache_cmp, cache_sel, win_state, page_table, nw, rel_bias, layer, col_blk=0):
    g_q, g_kc, g_ks, g_kw, pe2, w1k, w1v, w2k, w2v = nw
    b, n_pages = page_table.shape
    past = n_pages * PAGE_SIZE
    qn, cmp_rows, sel_rows, win_rows, gates = _nsa_prep(p, g_q, g_ks, g_kw, b, col_blk)
    q8 = jnp.pad(qn.reshape(b, NSA_HEADS, NSA_HEAD_DIM), ((0, 0), (0, 8 - NSA_HEADS), (0, 0)))
    kcmp, vcmp = _nsa_compress_paged(cache_cmp, page_table, layer, pe2, w1k, w1v, w2k, w2v, g_kc)
    o_cmp, ids = _nsa_cmp_select_decode(q8, kcmp, vcmp, _pair_matrix(past // CMP_BLOCK, past // SEL_BLOCK), rel_bias, past)
    o8, new_win = _nsa_sel_win_decode(q8, gates[:, None, :], o_cmp, sel_rows[:, None, :], win_rows[:, None, :],
                                      ids[:, 0, :N_SEL], page_table, cache_sel, win_state, rel_bias, layer, past)
    return o8[:, :NSA_HEADS].reshape(b, NSA_WIDTH), cmp_rows, sel_rows, new_win


def _moe(x, g, wr, br, w_gate, w_up, w_down, layer, tm_route, tm):
    h, route = _moe_router(x, g, wr, br, tm_route)
    eid = route[:, 0:2].astype(I32)
    wts = route[:, 2:4]
    row_src, row_w, slot_dest, tile_e, n_valid = _moe_dispatch(eid, wts, tm)
    xs = jnp.take(h, row_src, axis=0)
    ys = _moe_grouped(xs, row_w[:, None], tile_e, n_valid, w_gate, w_up, w_down, layer, tm)
    return x + jnp.take(ys, slot_dest[:, 0], axis=0) + jnp.take(ys, slot_dest[:, 1], axis=0)


P_COLS = 9 * MLA_P_COLS
NSA_COL_BLK = GDN_P_COLS // NSA_P_COLS
MLA_COL_BLK = P_COLS // MLA_P_COLS - 1
assert GDN_P_COLS % NSA_P_COLS == 0 and (NSA_COL_BLK + 1) * NSA_P_COLS <= MLA_COL_BLK * MLA_P_COLS


def _fused_w_in(w_in):
    n_gdn = 4 * GDN_WIDTH + 2 * GDN_HEADS
    n_nsa = NSA_WIDTH + 6 * NSA_HEAD_DIM + 3 * NSA_HEADS
    n_mla = MLA_Q_LORA + MLA_KV_LORA + MLA_ROPE
    assert w_in.shape[1] == n_gdn + n_nsa + n_mla
    w = jnp.zeros((w_in.shape[0], P_COLS), BF16)
    w = w.at[:, 0:n_gdn].set(w_in[:, 0:n_gdn].astype(BF16))
    w = w.at[:, GDN_P_COLS:GDN_P_COLS + n_nsa].set(w_in[:, n_gdn:n_gdn + n_nsa].astype(BF16))
    o = MLA_COL_BLK * MLA_P_COLS
    return w.at[:, o:o + n_mla].set(w_in[:, n_gdn + n_nsa:].astype(BF16))


def _layer_weights(l, norm1, w_in, gdn_conv, gdn_a_log, gdn_dt_bias, gdn_out_norm, nsa_q_norm, nsa_k_norm, nsa_cmp_pe,
                   nsa_cmp_w1, nsa_cmp_w2, mla_q_a_norm, mla_w_uq, mla_qk_norm, mla_kv_norm, mla_krope_norm, mla_w_uk,
                   mla_w_uv, w_out, norm2, moe_w_grp, moe_b_grp, moe_w_exp, moe_b_exp):
    lane8 = lambda v: jnp.zeros((1, LANES), F32).at[0, GDN_HEADS:2 * GDN_HEADS].set(v)
    wr = jnp.zeros((D_MODEL, LANES), F32).at[:, :N_GROUPS].set(moe_w_grp[l]).at[:, N_GROUPS:N_GROUPS + N_EXPERTS].set(moe_w_exp[l])
    br = jnp.zeros((1, LANES), F32).at[0, :N_GROUPS].set(moe_b_grp[l]).at[0, N_GROUPS:N_GROUPS + N_EXPERTS].set(moe_b_exp[l])
    return dict(
        norm1=norm1[l][None], w_in=_fused_w_in(w_in[l]),
        gdn=(gdn_conv[l], lane8(gdn_a_log[l]), lane8(gdn_dt_bias[l]), gdn_out_norm[l][None]),
        nsa=_nsa_weights(nsa_q_norm[l], nsa_k_norm[l], nsa_cmp_pe[l], nsa_cmp_w1[l], nsa_cmp_w2[l]),
        mla=_mla_weights(mla_q_a_norm[l], mla_w_uq[l], mla_qk_norm[l], mla_kv_norm[l], mla_krope_norm[l], mla_w_uk[l],
                         mla_w_uv[l]),
        w_out=w_out[l].astype(BF16), norm2=norm2[l][None], wr=wr, br=br)


def _prompt_layer(x, l, lw, rel_bias, moe_w):
    t = x.shape[0]
    p = _norm_matmul(x, lw["norm1"], lw["w_in"], 512, MLA_P_COLS)
    o_gdn, s_fin = _gdn_prompt(p, *lw["gdn"])
    new_conv = p[t - (CONV_W - 1):, 0:3 * GDN_WIDTH]
    o_nsa, cmp_rows, sel_rows, win_rows = _nsa_prompt(p, lw["nsa"], rel_bias, NSA_COL_BLK)
    cos, sin = _rope_tables(jnp.arange(t, dtype=I32))
    q, kv = _mla_prep(p, cos, sin, *lw["mla"][:-1], 512, MLA_COL_BLK)
    o_mla = _headwise_mm(_mla_flash(q, kv, 256, 512), lw["mla"][-1])
    x = _out_proj(x, o_gdn, o_nsa, o_mla, lw["w_out"], 512, 512)
    x = _moe(x, lw["norm2"], lw["wr"], lw["br"], *moe_w, l, 512, 256)
    return x, cmp_rows, sel_rows, kv, win_rows[t - min(NSA_WINDOW, t):], s_fin, new_conv


def _sample_layer(x, l, lw, rel_bias, moe_w, caches, states, page_table):
    b = x.shape[0]
    cache_cmp, cache_sel, cache_mla = caches
    win_state, gdn_state, conv_state = states
    past = page_table.shape[1] * PAGE_SIZE
    p = _norm_matmul(x, lw["norm1"], lw["w_in"], b, MLA_P_COLS)
    o_gdn, new_conv, new_gdn = _gdn_decode(p[:, None, :], conv_state, gdn_state, *lw["gdn"], l)
    o_nsa, cmp_rows, sel_rows, new_win = _nsa_decode(p, cache_cmp, cache_sel, win_state, page_table, lw["nsa"], rel_bias,
                                                     l, NSA_COL_BLK)
    cos, sin = _rope_tables(jnp.full((b,), past, I32))
    q, kv = _mla_prep(p, cos, sin, *lw["mla"][:-1], b, MLA_COL_BLK)
    q8 = jnp.pad(jnp.transpose(q, (1, 0, 2)), ((0, 0), (0, 8 - MLA_HEADS), (0, 0)))
    o_lat = _mla_decode(q8, kv[:, None, :], cache_mla, page_table, l)
    o_mla = _headwise_mm(jnp.transpose(o_lat[:, :MLA_HEADS], (1, 0, 2)), lw["mla"][-1])
    x = _out_proj(x, o_gdn[:, 0], o_nsa, o_mla, lw["w_out"], b, 512)
    x = _moe(x, lw["norm2"], lw["wr"], lw["br"], *moe_w, l, b, 32)
    return x, cmp_rows, sel_rows, kv, new_win, new_gdn, new_conv[:, 0]


def kernel(x_prompt, x_sample, cache_nsa_cmp, cache_nsa_sel, cache_mla, state_win_kv, state_gdn, state_conv, page_table, rel_bias, norm1, w_in, gdn_conv, gdn_a_log, gdn_dt_bias, gdn_out_norm, nsa_q_norm, nsa_k_norm, nsa_cmp_pe, nsa_cmp_w1, nsa_cmp_w2, mla_q_a_norm, mla_w_uq, mla_qk_norm, mla_kv_norm, mla_krope_norm, mla_w_uk, mla_w_uv, w_out, norm2, moe_w_grp, moe_b_grp, moe_w_exp, moe_b_exp, moe_w_gate, moe_w_up, moe_w_down):
    depth = norm1.shape[0]
    assert x_prompt.shape[0] == 1 and x_sample.shape[1] == 1
    moe_w = (moe_w_gate, moe_w_up, moe_w_down)
    lws = [_layer_weights(l, norm1, w_in, gdn_conv, gdn_a_log, gdn_dt_bias, gdn_out_norm, nsa_q_norm, nsa_k_norm,
                          nsa_cmp_pe, nsa_cmp_w1, nsa_cmp_w2, mla_q_a_norm, mla_w_uq, mla_qk_norm, mla_kv_norm,
                          mla_krope_norm, mla_w_uk, mla_w_uv, w_out, norm2, moe_w_grp, moe_b_grp, moe_w_exp, moe_b_exp)
           for l in range(depth)]

    xp = x_prompt[0]
    p_out = []
    for l in range(depth):
        xp, *rows = _prompt_layer(xp, l, lws[l], rel_bias, moe_w)
        p_out.append(rows)

    b = x_sample.shape[0]
    xs = x_sample[:, 0]
    conv_state = state_conv.reshape(b, depth, 1, (CONV_W - 1) * 3 * GDN_WIDTH)
    s_out = []
    for l in range(depth):
        xs, *rows = _sample_layer(xs, l, lws[l], rel_bias, moe_w, (cache_nsa_cmp, cache_nsa_sel, cache_mla),
                                  (state_win_kv, state_gdn, conv_state), page_table)
        s_out.append(rows)

    def stack_p(k):
        return jnp.stack([p_out[l][k] for l in range(depth)], axis=0)[None]

    def stack_s(k, shape=None):
        a = jnp.stack([s_out[l][k] for l in range(depth)], axis=1)
        return a if shape is None else a.reshape(shape)

    return (xp[None], xs[:, None],
            stack_p(0), stack_p(1), stack_p(2), stack_p(3), stack_p(4), stack_p(5),
            stack_s(0, (b, depth, 1, KV_W)), stack_s(1, (b, depth, 1, KV_W)), stack_s(2, (b, depth, 1, MLA_LAT)),
            stack_s(3), stack_s(4), stack_s(5, (b, depth, CONV_W - 1, 3 * GDN_WIDTH)))
```

```python
import functools
import math

import jax
import jax.numpy as jnp
import numpy as np
from jax import lax
from jax.experimental import pallas as pl
from jax.experimental.pallas import tpu as pltpu

F32, BF16, I32 = jnp.float32, jnp.bfloat16, jnp.int32
HI = lax.Precision.HIGHEST

D_MODEL = 2048
PAGE_SIZE = 128
GDN_HEADS, GDN_DK, GDN_DV = 8, 128, 128
GDN_WIDTH = GDN_HEADS * GDN_DV
CONV_W = 4
GDN_CHUNK = 64
NSA_HEADS, NSA_HEAD_DIM = 4, 128
NSA_WIDTH = NSA_HEADS * NSA_HEAD_DIM
CMP_BLOCK, CMP_HIDDEN, SEL_BLOCK, N_SEL, NSA_WINDOW = 32, 256, 64, 16, 512
MLA_HEADS, MLA_Q_LORA, MLA_KV_LORA, MLA_NOPE, MLA_ROPE, MLA_V = 4, 512, 128, 128, 32, 128
MLA_QK = MLA_NOPE + MLA_ROPE
MLA_LAT = MLA_KV_LORA + MLA_ROPE
ROPE_THETA = 10000.0
N_BUCKETS, MAX_DISTANCE = 32, 128
N_GROUPS, EXPERTS_PER_GROUP = 8, 8
N_EXPERTS = N_GROUPS * EXPERTS_PER_GROUP
D_EXPERT = 256
RMS_EPS = 1e-6
L2_EPS = 1e-6
Q_TILE = 128
FLASH_Q_TILE = 128
K_TILE = 256
LANES = 128
VMEM_LIMIT = 56 << 20


def _cparams(sem, vmem=VMEM_LIMIT):
    return pltpu.CompilerParams(dimension_semantics=sem, vmem_limit_bytes=vmem)


def _bdot(a, b):
    return jnp.dot(a.astype(BF16), b.astype(BF16), preferred_element_type=F32)


def _bdot_nt(a, b):
    return lax.dot_general(a.astype(BF16), b.astype(BF16), (((1,), (1,)), ((), ())), preferred_element_type=F32)


def _rms(x, g, n=None):
    n = x.shape[-1] if n is None else n
    return x * lax.rsqrt(jnp.sum(x * x, axis=-1, keepdims=True) / n + RMS_EPS) * g


def _sigmoid(x):
    return 1.0 / (1.0 + jnp.exp(-x))


def _t5_thresholds():
    n = np.arange(0, 4 * MAX_DISTANCE)
    exact = N_BUCKETS // 2
    out = []
    for dt in (np.float32, np.float64):
        nf = np.maximum(n, exact).astype(dt)
        large = exact + (np.log(nf / exact) / math.log(MAX_DISTANCE / exact) * (N_BUCKETS - exact)).astype(np.int32)
        b = np.where(n < exact, n, np.minimum(large, N_BUCKETS - 1))
        out.append([int(np.argmax(b >= k)) for k in range(1, N_BUCKETS)])
    assert out[0] == out[1]
    return out[0]


T5_THR = _t5_thresholds()


def _t5_bias(dist, tbl_ref, h):
    acc = jnp.full(dist.shape, tbl_ref[0, h], F32)
    for b in range(1, N_BUCKETS):
        acc = acc + jnp.where(dist >= T5_THR[b - 1], tbl_ref[b, h] - tbl_ref[b - 1, h], 0.0)
    return acc


def _causal_schedule(n_q, tq, tk):
    pairs = [(i, j) for i in range(n_q) for j in range((i * tq + tq - 1) // tk + 1)]
    return jnp.asarray([p[0] for p in pairs], I32), jnp.asarray([p[1] for p in pairs], I32)


def _gather_pipeline(step, n_steps, copies_of, n_copies, rolled=False):
    slot = step % 2

    def each(st, sl, op):
        if rolled:
            def one(k, carry):
                op(copies_of(st, sl, k))
                return carry
            lax.fori_loop(0, n_copies, one, 0)
        else:
            for k in range(n_copies):
                op(copies_of(st, sl, k))

    @pl.when(step == 0)
    def _():
        each(0, 0, lambda cp: cp.start())

    @pl.when(step + 1 < n_steps)
    def _():
        each(step + 1, 1 - slot, lambda cp: cp.start())

    @pl.when(step < n_steps)
    def _():
        each(step, slot, lambda cp: cp.wait())

    return slot


def _norm_matmul(x, g, w, tm, tn):
    m, d = x.shape
    n = w.shape[0]

    def body(x_ref, g_ref, w_ref, o_ref, h_ref):
        @pl.when(pl.program_id(1) == 0)
        def _():
            h_ref[...] = _rms(x_ref[...], g_ref[...]).astype(BF16)

        o_ref[...] = _bdot_nt(h_ref[...], w_ref[...])

    return pl.pallas_call(
        body, grid=(m // tm, n // tn),
        in_specs=[pl.BlockSpec((tm, d), lambda i, j: (i, 0)), pl.BlockSpec((1, d), lambda i, j: (0, 0)),
                  pl.BlockSpec((tn, d), lambda i, j: (j, 0))],
        out_specs=pl.BlockSpec((tm, tn), lambda i, j: (i, j)),
        out_shape=jax.ShapeDtypeStruct((m, n), F32),
        scratch_shapes=[pltpu.VMEM((tm, d), BF16)],
        compiler_params=_cparams(("arbitrary", "arbitrary")), name="norm_matmul")(x, g, w)


def _out_proj(x, a1, a2, a3, w, tm, tn):
    m, d = x.shape
    k1, k2, k3 = a1.shape[1], a2.shape[1], a3.shape[1]
    assert k1 % k2 == 0 and k2 == k3

    def body(x_ref, a1_ref, a2_ref, a3_ref, w1_ref, w2_ref, w3_ref, o_ref):
        o_ref[...] = (x_ref[...] + _bdot(a1_ref[...], w1_ref[...]) + _bdot(a2_ref[...], w2_ref[...])
                      + _bdot(a3_ref[...], w3_ref[...]))

    return pl.pallas_call(
        body, grid=(m // tm, d // tn),
        in_specs=[pl.BlockSpec((tm, tn), lambda i, j: (i, j)),
                  pl.BlockSpec((tm, k1), lambda i, j: (i, 0)), pl.BlockSpec((tm, k2), lambda i, j: (i, 0)),
                  pl.BlockSpec((tm, k3), lambda i, j: (i, 0)),
                  pl.BlockSpec((k1, tn), lambda i, j: (0, j)),
                  pl.BlockSpec((k2, tn), lambda i, j: (k1 // k2, j)),
                  pl.BlockSpec((k3, tn), lambda i, j: (k1 // k2 + 1, j))],
        out_specs=pl.BlockSpec((tm, tn), lambda i, j: (i, j)),
        out_shape=jax.ShapeDtypeStruct((m, d), F32),
        compiler_params=_cparams(("arbitrary", "arbitrary")), name="out_proj")(x, a1, a2, a3, w, w, w)


def _headwise_mm(x, w):
    h, m, k = x.shape
    n = w.shape[2]
    tm = min(m, 512)

    def body(x_ref, w_ref, o_ref):
        for i in range(h):
            o_ref[:, i * n:(i + 1) * n] = _bdot(x_ref[i], w_ref[i])

    return pl.pallas_call(
        body, grid=(m // tm,),
        in_specs=[pl.BlockSpec((h, tm, k), lambda i: (0, i, 0)), pl.BlockSpec((h, k, n), lambda i: (0, 0, 0))],
        out_specs=pl.BlockSpec((tm, h * n), lambda i: (i, 0)),
        out_shape=jax.ShapeDtypeStruct((m, h * n), F32),
        compiler_params=_cparams(("arbitrary",)), name="headwise_mm")(x, w)


def _moe_router(x, g, wr, br, tm):
    m, d = x.shape

    def body(x_ref, g_ref, wr_ref, br_ref, h_ref, r_ref):
        h = _rms(x_ref[...], g_ref[...])
        h_ref[...] = h
        lg = jnp.dot(h, wr_ref[...], precision=HI, preferred_element_type=F32) + br_ref[...]
        lane = lax.broadcasted_iota(I32, lg.shape, 1)
        is_g = lane < N_GROUPS
        lgm = jnp.where(is_g, lg, -jnp.inf)
        mg = jnp.max(lgm, axis=-1, keepdims=True)
        p_top = 1.0 / jnp.sum(jnp.where(is_g, jnp.exp(lgm - mg), 0.0), axis=-1, keepdims=True)
        gidx = jnp.min(jnp.where(lgm == mg, lane, LANES), axis=-1, keepdims=True)
        in_grp = (lane >= N_GROUPS) & (lane < N_GROUPS + N_EXPERTS) & (((lane - N_GROUPS) // EXPERTS_PER_GROUP) == gidx)
        le = jnp.where(in_grp, lg, -jnp.inf)
        m1 = jnp.max(le, axis=-1, keepdims=True)
        i1 = jnp.min(jnp.where(le == m1, lane, LANES), axis=-1, keepdims=True)
        le2 = jnp.where(lane == i1, -jnp.inf, le)
        m2 = jnp.max(le2, axis=-1, keepdims=True)
        i2 = jnp.min(jnp.where(le2 == m2, lane, LANES), axis=-1, keepdims=True)
        e2 = jnp.exp(m2 - m1)
        w1 = p_top / (1.0 + e2)
        w2 = p_top * e2 / (1.0 + e2)
        r_ref[...] = jnp.where(lane == 0, (i1 - N_GROUPS).astype(F32),
                               jnp.where(lane == 1, (i2 - N_GROUPS).astype(F32),
                                         jnp.where(lane == 2, w1, jnp.where(lane == 3, w2, 0.0))))

    return pl.pallas_call(
        body, grid=(m // tm,),
        in_specs=[pl.BlockSpec((tm, d), lambda i: (i, 0)), pl.BlockSpec((1, d), lambda i: (0, 0)),
                  pl.BlockSpec((d, LANES), lambda i: (0, 0)), pl.BlockSpec((1, LANES), lambda i: (0, 0))],
        out_specs=[pl.BlockSpec((tm, d), lambda i: (i, 0)), pl.BlockSpec((tm, LANES), lambda i: (i, 0))],
        out_shape=[jax.ShapeDtypeStruct((m, d), F32), jax.ShapeDtypeStruct((m, LANES), F32)],
        compiler_params=_cparams(("arbitrary",)), name="moe_router")(x, g, wr, br)


def _moe_dispatch(eid, wts, tm):
    t = eid.shape[0]
    n = 2 * t
    n_tiles = n // tm + N_EXPERTS
    e_flat = eid.reshape(n)
    order = jnp.argsort(e_flat, stable=True)
    e_s = e_flat[order]
    counts = jnp.sum(jax.nn.one_hot(e_flat, N_EXPERTS, dtype=I32), axis=0)
    tiles_e = (counts + tm - 1) // tm
    tile_end = jnp.cumsum(tiles_e)
    grp_start = jnp.cumsum(counts) - counts
    dest_sorted = (tile_end - tiles_e)[e_s] * tm + (jnp.arange(n, dtype=I32) - grp_start[e_s])
    row_src = jnp.zeros((n_tiles * tm,), I32).at[dest_sorted].set((order // 2).astype(I32))
    row_w = jnp.zeros((n_tiles * tm,), F32).at[dest_sorted].set(wts.reshape(n)[order])
    slot_dest = jnp.zeros((n,), I32).at[order].set(dest_sorted).reshape(t, 2)
    n_valid = tile_end[-1]
    ti = jnp.arange(n_tiles, dtype=I32)
    tile_e = jnp.minimum(jnp.searchsorted(tile_end, ti, side="right").astype(I32), N_EXPERTS - 1)
    tile_e = jnp.where(ti < n_valid, tile_e, tile_e[n_valid - 1])
    return row_src, row_w, slot_dest, tile_e, n_valid.reshape(1).astype(I32)


def _moe_grouped(h, row_src, ws, tile_e, n_valid, w_gate, w_up, w_down, layer, tm):
    d = h.shape[1]
    r = row_src.shape[0]
    f = w_gate.shape[-1]

    def body(te_ref, nv_ref, rs_ref, h_ref, w_ref, wg_ref, wu_ref, wd_ref, o_ref, xbuf, sem):
        def row_copy(st, sl, k):
            return pltpu.make_async_copy(h_ref.at[pl.ds(rs_ref[st * tm + k], 1)], xbuf.at[sl, pl.ds(k, 1)], sem.at[sl])

        slot = _gather_pipeline(pl.program_id(0), nv_ref[0], row_copy, tm, rolled=True)

        @pl.when(pl.program_id(0) < nv_ref[0])
        def _():
            x = xbuf[slot]
            a = _bdot(x, wg_ref[...])
            b = _bdot(x, wu_ref[...])
            act = a * _sigmoid(a) * b * w_ref[...]
            o_ref[...] = _bdot(act, wd_ref[...])

        @pl.when(pl.program_id(0) >= nv_ref[0])
        def _():
            o_ref[...] = jnp.zeros(o_ref.shape, F32)

    def wmap(i, te, nv, rs):
        return (layer, te[i], 0, 0)

    gs = pltpu.PrefetchScalarGridSpec(
        num_scalar_prefetch=3, grid=(r // tm,),
        in_specs=[pl.BlockSpec(memory_space=pl.ANY),
                  pl.BlockSpec((tm, 1), lambda i, te, nv, rs: (jnp.minimum(i, nv[0] - 1), 0)),
                  pl.BlockSpec((None, None, d, f), wmap), pl.BlockSpec((None, None, d, f), wmap),
                  pl.BlockSpec((None, None, f, d), wmap)],
        out_specs=pl.BlockSpec((tm, d), lambda i, te, nv, rs: (i, 0)),
        scratch_shapes=[pltpu.VMEM((2, tm, d), F32), pltpu.SemaphoreType.DMA((2,))])
    return pl.pallas_call(body, grid_spec=gs, out_shape=jax.ShapeDtypeStruct((r, d), F32),
                          compiler_params=_cparams(("arbitrary",)), name="moe_grouped")(
        tile_e, n_valid, row_src, h, ws, w_gate, w_up, w_down)


def _moe_combine(x, ys, slot_dest, tc):
    t, d = x.shape
    n_steps = t // tc
    dest = slot_dest.reshape(n_steps, tc, 2).transpose(0, 2, 1).reshape(-1)

    def body(d_ref, x_ref, ys_ref, o_ref, buf, sem):
        def row_copy(st, sl, k):
            return pltpu.make_async_copy(ys_ref.at[pl.ds(d_ref[st * 2 * tc + k], 1)], buf.at[sl, pl.ds(k, 1)], sem.at[sl])

        slot = _gather_pipeline(pl.program_id(0), n_steps, row_copy, 2 * tc, rolled=True)
        o_ref[...] = x_ref[...] + buf[slot, 0:tc, :] + buf[slot, tc:2 * tc, :]

    gs = pltpu.PrefetchScalarGridSpec(
        num_scalar_prefetch=1, grid=(n_steps,),
        in_specs=[pl.BlockSpec((tc, d), lambda i, dr: (i, 0)), pl.BlockSpec(memory_space=pl.ANY)],
        out_specs=pl.BlockSpec((tc, d), lambda i, dr: (i, 0)),
        scratch_shapes=[pltpu.VMEM((2, 2 * tc, d), F32), pltpu.SemaphoreType.DMA((2,))])
    return pl.pallas_call(body, grid_spec=gs, out_shape=jax.ShapeDtypeStruct((t, d), F32),
                          compiler_params=_cparams(("arbitrary",)), name="moe_combine")(dest, x, ys)


GDN_P_COLS = 4 * GDN_WIDTH + LANES


def _softplus(x):
    return jnp.maximum(x, 0.0) + jnp.log(1.0 + jnp.exp(-jnp.abs(x)))


def _gdn_gates(ba, al, dtb):
    return _sigmoid(ba), -jnp.exp(al) * _softplus(ba + dtb)


def _gdn_prompt(p, conv_w, a_log, dt_bias, out_norm):
    t = p.shape[0]
    c, nh, dk = GDN_CHUNK, GDN_HEADS, GDN_DK
    w3 = 3 * GDN_WIDTH

    def body(qkv_ref, z_ref, ba_ref, cw_ref, al_ref, dtb_ref, gn_ref, o_ref, sfin_ref, xbuf, s_ref):
        i = pl.program_id(0)

        @pl.when(i == 0)
        def _():
            xbuf[0:8, :] = jnp.zeros((8, w3), F32)
            s_ref[...] = jnp.zeros(s_ref.shape, F32)

        xbuf[8:8 + c, :] = qkv_ref[...]
        acc = cw_ref[0:1, :] * xbuf[5:5 + c, :]
        for j in range(1, CONV_W):
            acc = acc + cw_ref[j:j + 1, :] * xbuf[5 + j:5 + j + c, :]
        xbuf[0:8, :] = xbuf[c:c + 8, :]
        qkv = acc * _sigmoid(acc)

        def heads(off):
            return jnp.stack([qkv[:, off + h * dk: off + (h + 1) * dk] for h in range(nh)], axis=0)

        q, k, v = heads(0), heads(GDN_WIDTH), heads(2 * GDN_WIDTH)
        q = q * lax.rsqrt(jnp.sum(q * q, axis=-1, keepdims=True) + L2_EPS) * (dk ** -0.5)
        k = k * lax.rsqrt(jnp.sum(k * k, axis=-1, keepdims=True) + L2_EPS)
        beta_l, g_l = _gdn_gates(ba_ref[...], al_ref[...], dtb_ref[...])
        row = lax.broadcasted_iota(I32, (c, c), 0)
        col = lax.broadcasted_iota(I32, (c, c), 1)
        tril = (row >= col).astype(F32)
        gc_l = jnp.dot(tril, g_l, precision=HI, preferred_element_type=F32)
        beta = jnp.stack([beta_l[:, h:h + 1] for h in range(nh)], axis=0)
        gcol = jnp.stack([gc_l[:, nh + h:nh + h + 1] for h in range(nh)], axis=0)
        eye = (row == col)[None]
        grow = jnp.sum(jnp.where(eye, gcol, 0.0), axis=1, keepdims=True)
        glast = gcol[:, c - 1:c, :]
        causal = (row >= col)[None]
        strict = (row > col)[None]
        decay = jnp.exp(jnp.where(causal, gcol - grow, -jnp.inf))
        eg = jnp.exp(gcol)
        kb = k * beta
        kk = jnp.einsum("hid,hjd->hij", kb.astype(BF16), k.astype(BF16), preferred_element_type=F32)
        nmat = -jnp.where(strict, kk * decay, 0.0)
        tmat = jnp.where(eye, 1.0, 0.0) + nmat
        pw = nmat
        for _ in range(5):
            pw = jnp.einsum("hij,hjk->hik", pw, pw, precision=HI, preferred_element_type=F32)
            tmat = tmat + jnp.einsum("hij,hjk->hik", tmat, pw, precision=HI, preferred_element_type=F32)
        tb = tmat.astype(BF16)
        u = jnp.einsum("hij,hjd->hid", tb, (v * beta).astype(BF16), preferred_element_type=F32)
        w = jnp.einsum("hij,hjd->hid", tb, (kb * eg).astype(BF16), preferred_element_type=F32)
        attn = jnp.einsum("hid,hjd->hij", q.astype(BF16), k.astype(BF16), preferred_element_type=F32) * decay
        qg = q * eg
        kd = k * jnp.exp(glast - gcol)
        s = s_ref[...]
        sb = s.astype(BF16)
        v_new = u - jnp.einsum("hik,hkd->hid", w.astype(BF16), sb, preferred_element_type=F32)
        vb = v_new.astype(BF16)
        o = (jnp.einsum("hik,hkd->hid", qg.astype(BF16), sb, preferred_element_type=F32)
             + jnp.einsum("hij,hjd->hid", attn.astype(BF16), vb, preferred_element_type=F32))
        s_new = s * jnp.exp(glast) + jnp.einsum("hik,hid->hkd", kd.astype(BF16), vb, preferred_element_type=F32)
        s_ref[...] = s_new
        sfin_ref[...] = s_new
        z = z_ref[...]
        for h in range(nh):
            zh = z[:, h * dk:(h + 1) * dk]
            o_ref[:, h * dk:(h + 1) * dk] = _rms(o[h], gn_ref[...]) * (zh * _sigmoid(zh))

    return pl.pallas_call(
        body, grid=(t // c,),
        in_specs=[pl.BlockSpec((c, w3), lambda i: (i, 0)), pl.BlockSpec((c, GDN_WIDTH), lambda i: (i, 3)),
                  pl.BlockSpec((c, LANES), lambda i: (i, 4 * GDN_WIDTH // LANES)),
                  pl.BlockSpec((CONV_W, w3), lambda i: (0, 0)),
                  pl.BlockSpec((1, LANES), lambda i: (0, 0)), pl.BlockSpec((1, LANES), lambda i: (0, 0)),
                  pl.BlockSpec((1, GDN_DV), lambda i: (0, 0))],
        out_specs=[pl.BlockSpec((c, GDN_WIDTH), lambda i: (i, 0)),
                   pl.BlockSpec((nh, dk, GDN_DV), lambda i: (0, 0, 0))],
        out_shape=[jax.ShapeDtypeStruct((t, GDN_WIDTH), F32), jax.ShapeDtypeStruct((nh, dk, GDN_DV), F32)],
        scratch_shapes=[pltpu.VMEM((c + 8, w3), F32), pltpu.VMEM((nh, dk, GDN_DV), F32)],
        compiler_params=_cparams(("arbitrary",)), name="gdn_prompt")(p, p, p, conv_w, a_log, dt_bias, out_norm)


def _gdn_decode(p3, conv_state, state, conv_w, a_log, dt_bias, out_norm, layer, earlier):
    b = p3.shape[0]
    nh, dk = GDN_HEADS, GDN_DK
    w3 = 3 * GDN_WIDTH
    ne = layer if earlier else 0
    assert ne == layer

    def body(qkv_ref, z_ref, ba_ref, cs_ref, st_ref, cw_ref, al_ref, dtb_ref, gn_ref, *rest):
        early, (o_ref, ncs_ref, nst_ref) = rest[:len(earlier)], rest[len(earlier):]
        if earlier:
            ncs_ref[:, 0:ne, :] = early[0][...]
            nst_ref[0:ne] = early[1][...]
        x = qkv_ref[...]
        rows = [cs_ref[j, layer:layer + 1, :] for j in range(CONV_W - 1)] + [x]
        acc = cw_ref[0:1, :] * rows[0]
        for j in range(1, CONV_W):
            acc = acc + cw_ref[j:j + 1, :] * rows[j]
        for j in range(CONV_W - 1):
            ncs_ref[j, ne:ne + 1, :] = rows[j + 1]
        qkv = acc * _sigmoid(acc)

        def heads(off):
            return jnp.concatenate([qkv[:, off + h * dk: off + (h + 1) * dk] for h in range(nh)], axis=0)

        q, k, v = heads(0), heads(GDN_WIDTH), heads(2 * GDN_WIDTH)
        q = q * lax.rsqrt(jnp.sum(q * q, axis=-1, keepdims=True) + L2_EPS) * (dk ** -0.5)
        k = k * lax.rsqrt(jnp.sum(k * k, axis=-1, keepdims=True) + L2_EPS)
        qk = jnp.sum(q * k, axis=-1, keepdims=True)
        beta_l, g_l = _gdn_gates(ba_ref[...], al_ref[...], dtb_ref[...])
        kt = k.T
        qt = q.T
        z = z_ref[...]
        for h in range(nh):
            s = st_ref[h]
            bh = beta_l[:, h:h + 1]
            egh = jnp.exp(g_l[:, nh + h:nh + h + 1])
            kc = kt[:, h:h + 1]
            qc = qt[:, h:h + 1]
            ks = jnp.sum(kc * s, axis=0, keepdims=True)
            qs = jnp.sum(qc * s, axis=0, keepdims=True)
            v_new = v[h:h + 1, :] * bh - ks * (bh * egh)
            o = qs * egh + qk[h:h + 1, :] * v_new
            nst_ref[ne, h] = s * egh + kc * v_new
            zh = z[:, h * dk:(h + 1) * dk]
            o_ref[:, h * dk:(h + 1) * dk] = _rms(o, gn_ref[...]) * (zh * _sigmoid(zh))

    per_sample = lambda shape: pl.BlockSpec((None,) + shape, lambda i: (i,) + (0,) * len(shape))
    conv_out, state_out = (CONV_W - 1, ne + 1, w3), (ne + 1, nh, dk, GDN_DV)
    return pl.pallas_call(
        body, grid=(b,),
        in_specs=[pl.BlockSpec((None, 1, w3), lambda i: (i, 0, 0)),
                  pl.BlockSpec((None, 1, GDN_WIDTH), lambda i: (i, 0, 3)),
                  pl.BlockSpec((None, 1, LANES), lambda i: (i, 0, 4 * GDN_WIDTH // LANES)),
                  per_sample(conv_state.shape[1:]),
                  pl.BlockSpec((None, None, nh, dk, GDN_DV), lambda i: (i, layer, 0, 0, 0)),
                  pl.BlockSpec((CONV_W, w3), lambda i: (0, 0)),
                  pl.BlockSpec((1, LANES), lambda i: (0, 0)), pl.BlockSpec((1, LANES), lambda i: (0, 0)),
                  pl.BlockSpec((1, GDN_DV), lambda i: (0, 0))] + [per_sample(a.shape[1:]) for a in earlier],
        out_specs=[pl.BlockSpec((None, 1, GDN_WIDTH), lambda i: (i, 0, 0)), per_sample(conv_out), per_sample(state_out)],
        out_shape=[jax.ShapeDtypeStruct((b, 1, GDN_WIDTH), F32), jax.ShapeDtypeStruct((b,) + conv_out, F32),
                   jax.ShapeDtypeStruct((b,) + state_out, F32)],
        compiler_params=_cparams(("arbitrary",)), name="gdn_decode")(
        p3, p3, p3, conv_state, state, conv_w, a_log, dt_bias, out_norm, *earlier)


MLA_P_COLS = MLA_Q_LORA + 2 * LANES


def _rope_tables(pos):
    half = MLA_ROPE // 2
    inv = ROPE_THETA ** (-jnp.arange(half, dtype=F32) / half)
    ang = jnp.tile(pos.astype(F32)[:, None] * inv, (1, LANES // half))
    sign = jnp.where((jnp.arange(LANES) % MLA_ROPE) < half, -1.0, 1.0).astype(F32)
    return jnp.cos(ang), jnp.sin(ang) * sign


def _mla_weights(q_a_norm, w_uq, qk_norm, kv_norm, krope_norm, w_uk, w_uv):
    nh = MLA_HEADS
    w4 = w_uq.reshape(MLA_Q_LORA, nh, MLA_QK)
    w_uq_p = jnp.concatenate([w4[:, :, :MLA_NOPE].reshape(MLA_Q_LORA, nh * MLA_NOPE),
                              w4[:, :, MLA_NOPE:].reshape(MLA_Q_LORA, nh * MLA_ROPE)], axis=1).astype(BF16)
    g_nope = qk_norm[None, :MLA_NOPE]
    g_rope = jnp.tile(qk_norm[MLA_NOPE:], nh)[None]
    w_uk_t = jnp.transpose(w_uk, (1, 2, 0)).astype(BF16)
    w_uv_t = jnp.transpose(w_uv, (1, 0, 2)).astype(BF16)
    g_kr = jnp.pad(krope_norm, (0, LANES - MLA_ROPE))[None]
    return q_a_norm[None], w_uq_p, g_nope, g_rope, w_uk_t, kv_norm[None], g_kr, w_uv_t


def _rope_rotate(x, cos, sin_signed):
    lane = lax.broadcasted_iota(I32, x.shape, 1)
    half = MLA_ROPE // 2
    swapped = jnp.where(lane % MLA_ROPE < half, pltpu.roll(x, LANES - half, 1), pltpu.roll(x, half, 1))
    return x * cos + swapped * sin_signed


def _mla_prep(p, cos, sin, g_qa, w_uq, g_nope, g_rope, w_uk, g_kv, g_kr, tm, col_blk=0):
    m = p.shape[0]
    nh = MLA_HEADS

    def body(p_ref, cos_ref, sin_ref, gqa_ref, wuq_ref, gn_ref, gr_ref, wuk_ref, gkv_ref, gkr_ref, q_ref, kv_ref):
        cos, sin = cos_ref[...], sin_ref[...]
        cq = _rms(p_ref[:, 0:MLA_Q_LORA], gqa_ref[...])
        qf = _bdot(cq, wuq_ref[...])
        rope_all = qf[:, nh * MLA_NOPE:]
        r2 = rope_all * rope_all
        lane = lax.broadcasted_iota(I32, rope_all.shape, 1)
        inv_lane = jnp.zeros(rope_all.shape, F32)
        for h in range(nh):
            nope = qf[:, h * MLA_NOPE:(h + 1) * MLA_NOPE]
            in_h = lane // MLA_ROPE == h
            ss = jnp.sum(nope * nope, axis=-1, keepdims=True) + jnp.sum(jnp.where(in_h, r2, 0.0), axis=-1, keepdims=True)
            inv = lax.rsqrt(ss / MLA_QK + RMS_EPS)
            inv_lane = jnp.where(in_h, inv, inv_lane)
            q_ref[h, :, 0:MLA_KV_LORA] = _bdot(nope * inv * gn_ref[...], wuk_ref[h])
        rot = _rope_rotate(rope_all * inv_lane * gr_ref[...], cos, sin)
        for h in range(nh):
            q_ref[h, :, MLA_KV_LORA:MLA_LAT] = rot[:, h * MLA_ROPE:(h + 1) * MLA_ROPE]
        kv_ref[:, 0:MLA_KV_LORA] = _rms(p_ref[:, MLA_Q_LORA:MLA_Q_LORA + MLA_KV_LORA], gkv_ref[...])
        kr = p_ref[:, MLA_Q_LORA + MLA_KV_LORA:MLA_P_COLS]
        kv_ref[:, MLA_KV_LORA:MLA_LAT] = _rope_rotate(_rms(kr, gkr_ref[...], MLA_ROPE), cos, sin)[:, 0:MLA_ROPE]

    full = lambda a: pl.BlockSpec(a.shape, lambda i: (0,) * a.ndim)
    return pl.pallas_call(
        body, grid=(m // tm,),
        in_specs=[pl.BlockSpec((tm, MLA_P_COLS), lambda i: (i, col_blk)), pl.BlockSpec((tm, LANES), lambda i: (i, 0)),
                  pl.BlockSpec((tm, LANES), lambda i: (i, 0)), full(g_qa), full(w_uq), full(g_nope), full(g_rope),
                  full(w_uk), full(g_kv), full(g_kr)],
        out_specs=[pl.BlockSpec((nh, tm, MLA_LAT), lambda i: (0, i, 0)), pl.BlockSpec((tm, MLA_LAT), lambda i: (i, 0))],
        out_shape=[jax.ShapeDtypeStruct((nh, m, MLA_LAT), F32), jax.ShapeDtypeStruct((m, MLA_LAT), F32)],
        compiler_params=_cparams(("arbitrary",)), name="mla_prep")(
        p, cos, sin, g_qa, w_uq, g_nope, g_rope, w_uk, g_kv, g_kr)


def _mla_flash(q, kv, tq, tk):
    nh, t, _ = q.shape
    scale = MLA_QK ** -0.5
    qi, kj = _causal_schedule(t // tq, tq, tk)

    def body(qi_ref, kj_ref, q_ref, kv_ref, o_ref, qs_ref, m_ref, l_ref, acc_ref):
        i, j = qi_ref[pl.program_id(0)], kj_ref[pl.program_id(0)]

        @pl.when(j == 0)
        def _():
            qs_ref[...] = (q_ref[...].reshape(nh * tq, MLA_LAT) * scale).astype(BF16)
            m_ref[...] = jnp.full(m_ref.shape, -jnp.inf, F32)
            l_ref[...] = jnp.zeros(l_ref.shape, F32)
            acc_ref[...] = jnp.zeros(acc_ref.shape, F32)

        kb = kv_ref[...].astype(BF16)
        rel = (i * tq - j * tk) + lax.broadcasted_iota(I32, (tq, tk), 0) - lax.broadcasted_iota(I32, (tq, tk), 1)
        causal = jnp.where(rel >= 0, 0.0, -jnp.inf)
        s = (_bdot_nt(qs_ref[...], kb).reshape(nh, tq, tk) + causal[None]).reshape(nh * tq, tk)
        m_new = jnp.maximum(m_ref[...], jnp.max(s, axis=-1, keepdims=True))
        alpha = jnp.exp(m_ref[...] - m_new)
        pr = jnp.exp(s - m_new)
        l_ref[...] = l_ref[...] * alpha + jnp.sum(pr, axis=-1, keepdims=True)
        acc_ref[...] = acc_ref[...] * alpha + jnp.dot(pr.astype(BF16), kb[:, 0:MLA_KV_LORA], preferred_element_type=F32)
        m_ref[...] = m_new

        @pl.when(j == (i * tq + tq - 1) // tk)
        def _():
            o_ref[...] = (acc_ref[...] / l_ref[...]).reshape(nh, tq, MLA_KV_LORA)

    gs = pltpu.PrefetchScalarGridSpec(
        num_scalar_prefetch=2, grid=(qi.shape[0],),
        in_specs=[pl.BlockSpec((nh, tq, MLA_LAT), lambda s, qi, kj: (0, qi[s], 0)),
                  pl.BlockSpec((tk, MLA_LAT), lambda s, qi, kj: (kj[s], 0))],
        out_specs=pl.BlockSpec((nh, tq, MLA_KV_LORA), lambda s, qi, kj: (0, qi[s], 0)),
        scratch_shapes=[pltpu.VMEM((nh * tq, MLA_LAT), BF16), pltpu.VMEM((nh * tq, 1), F32),
                        pltpu.VMEM((nh * tq, 1), F32), pltpu.VMEM((nh * tq, MLA_KV_LORA), F32)])
    return pl.pallas_call(body, grid_spec=gs, out_shape=jax.ShapeDtypeStruct((nh, t, MLA_KV_LORA), F32),
                          compiler_params=_cparams(("arbitrary",)), name="mla_flash")(qi, kj, q, kv)


PAGES_PER_STEP = 64


def _mla_decode(q8, knew, cache, page_table, layer):
    b, n_pages = page_table.shape
    pp = min(PAGES_PER_STEP, n_pages)
    nc = n_pages // pp
    scale = MLA_QK ** -0.5

    def body(pt_ref, q_ref, kn_ref, cache_ref, o_ref, buf, sem, m_ref, l_ref, acc_ref):
        c = pl.program_id(1)
        step = pl.program_id(0) * nc + c

        def page_copy(st, sl, jj):
            page = pt_ref[st // nc, (st % nc) * pp + jj]
            return pltpu.make_async_copy(cache_ref.at[page, layer], buf.at[sl, jj], sem.at[sl])

        slot = _gather_pipeline(step, b * nc, page_copy, pp)

        @pl.when(c == 0)
        def _():
            m_ref[...] = jnp.full(m_ref.shape, -jnp.inf, F32)
            l_ref[...] = jnp.zeros(l_ref.shape, F32)
            acc_ref[...] = jnp.zeros(acc_ref.shape, F32)

        qb = q_ref[...].astype(BF16)
        kbs = [buf[slot, jj].astype(BF16) for jj in range(pp)]
        s = jnp.concatenate([jnp.dot(qb, kb, preferred_element_type=F32) for kb in kbs], axis=1) * scale
        m_new = jnp.maximum(m_ref[...], jnp.max(s, axis=-1, keepdims=True))
        alpha = jnp.exp(m_ref[...] - m_new)
        pr = jnp.exp(s - m_new)
        l_ref[...] = l_ref[...] * alpha + jnp.sum(pr, axis=-1, keepdims=True)
        pv = acc_ref[...] * alpha
        for jj, kb in enumerate(kbs):
            pv = pv + _bdot_nt(pr[:, jj * PAGE_SIZE:(jj + 1) * PAGE_SIZE], kb[0:MLA_KV_LORA, :])
        acc_ref[...] = pv
        m_ref[...] = m_new

        @pl.when(c == nc - 1)
        def _():
            kn = kn_ref[...]
            sn = jnp.sum(q_ref[...] * kn, axis=-1, keepdims=True) * scale
            m2 = jnp.maximum(m_ref[...], sn)
            a2 = jnp.exp(m_ref[...] - m2)
            pn = jnp.exp(sn - m2)
            l2 = l_ref[...] * a2 + pn
            o_ref[...] = (acc_ref[...] * a2 + pn * kn[:, 0:MLA_KV_LORA]) / l2

    gs = pltpu.PrefetchScalarGridSpec(
        num_scalar_prefetch=1, grid=(b, nc),
        in_specs=[pl.BlockSpec((None, 8, MLA_LAT), lambda i, c, pt: (i, 0, 0)),
                  pl.BlockSpec((None, 1, MLA_LAT), lambda i, c, pt: (i, 0, 0)), pl.BlockSpec(memory_space=pl.ANY)],
        out_specs=pl.BlockSpec((None, 8, MLA_KV_LORA), lambda i, c, pt: (i, 0, 0)),
        scratch_shapes=[pltpu.VMEM((2, pp, MLA_LAT, PAGE_SIZE), F32), pltpu.SemaphoreType.DMA((2,)),
                        pltpu.VMEM((8, 1), F32), pltpu.VMEM((8, 1), F32), pltpu.VMEM((8, MLA_KV_LORA), F32)])
    return pl.pallas_call(body, grid_spec=gs, out_shape=jax.ShapeDtypeStruct((b, 8, MLA_KV_LORA), F32),
                          compiler_params=_cparams(("arbitrary", "arbitrary")), name="mla_decode")(
        page_table, q8, knew, cache)


NSA_P_COLS = NSA_WIDTH + 7 * LANES
KV_W = 2 * NSA_HEAD_DIM


def _nsa_prep(p, g_q, g_ksel, g_kwin, tm, col_blk=0):
    m = p.shape[0]
    dh = NSA_HEAD_DIM

    def body(p_ref, gq_ref, gs_ref, gw_ref, q_ref, c_ref, s_ref, w_ref, g_ref):
        for h in range(NSA_HEADS):
            q_ref[:, h * dh:(h + 1) * dh] = _rms(p_ref[:, h * dh:(h + 1) * dh], gq_ref[...])
        o = NSA_WIDTH
        c_ref[...] = p_ref[:, o:o + 2 * dh]
        s_ref[:, 0:dh] = _rms(p_ref[:, o + 2 * dh:o + 3 * dh], gs_ref[...])
        s_ref[:, dh:2 * dh] = p_ref[:, o + 3 * dh:o + 4 * dh]
        w_ref[:, 0:dh] = _rms(p_ref[:, o + 4 * dh:o + 5 * dh], gw_ref[...])
        w_ref[:, dh:2 * dh] = p_ref[:, o + 5 * dh:o + 6 * dh]
        g_ref[...] = _sigmoid(p_ref[:, o + 6 * dh:o + 7 * dh])

    row = lambda w: pl.BlockSpec((tm, w), lambda i: (i, 0))
    vec = pl.BlockSpec((1, dh), lambda i: (0, 0))
    return pl.pallas_call(
        body, grid=(m // tm,), in_specs=[pl.BlockSpec((tm, NSA_P_COLS), lambda i: (i, col_blk)), vec, vec, vec],
        out_specs=[row(NSA_WIDTH), row(KV_W), row(KV_W), row(KV_W), row(LANES)],
        out_shape=[jax.ShapeDtypeStruct((m, w), F32) for w in (NSA_WIDTH, KV_W, KV_W, KV_W, LANES)],
        compiler_params=_cparams(("arbitrary",)), name="nsa_prep")(p, g_q, g_ksel, g_kwin)


def _gelu_tanh(x):
    return 0.5 * x * (1.0 + jnp.tanh(math.sqrt(2.0 / math.pi) * (x + 0.044715 * x * x * x)))


def _compress_tail(acc_k, acc_v, w2k_ref, w2v_ref, gk_ref):
    kc = _rms(_bdot(_gelu_tanh(acc_k), w2k_ref[...]), gk_ref[...])
    vc = _bdot(_gelu_tanh(acc_v), w2v_ref[...])
    return kc, vc


def _nsa_compress(xb, pe, w1k, w1v, w2k, w2v, g_k):
    nb = xb.shape[0]
    bm = min(nb, 256)
    dh = NSA_HEAD_DIM

    def body(x_ref, pe_ref, w1k_ref, w1v_ref, w2k_ref, w2v_ref, gk_ref, kc_ref, vc_ref):
        x = x_ref[...] + pe_ref[...]
        xk = jnp.concatenate([x[:, r * KV_W:r * KV_W + dh] for r in range(CMP_BLOCK)], axis=1)
        xv = jnp.concatenate([x[:, r * KV_W + dh:(r + 1) * KV_W] for r in range(CMP_BLOCK)], axis=1)
        kc, vc = _compress_tail(_bdot(xk, w1k_ref[...]), _bdot(xv, w1v_ref[...]), w2k_ref, w2v_ref, gk_ref)
        kc_ref[...] = kc
        vc_ref[...] = vc

    full = lambda a: pl.BlockSpec(a.shape, lambda i: (0,) * a.ndim)
    return pl.pallas_call(
        body, grid=(nb // bm,),
        in_specs=[pl.BlockSpec((bm, CMP_BLOCK * KV_W), lambda i: (i, 0)), full(pe), full(w1k), full(w1v), full(w2k),
                  full(w2v), full(g_k)],
        out_specs=[pl.BlockSpec((bm, dh), lambda i: (i, 0)), pl.BlockSpec((bm, dh), lambda i: (i, 0))],
        out_shape=[jax.ShapeDtypeStruct((nb, dh), F32), jax.ShapeDtypeStruct((nb, dh), F32)],
        compiler_params=_cparams(("arbitrary",)), name="nsa_compress")(xb, pe, w1k, w1v, w2k, w2v, g_k)


CMP_PAGES = 64
BLOCKS_PER_PAGE = PAGE_SIZE // CMP_BLOCK
PAGE_PITCH = PAGE_SIZE + 8


def _nsa_compress_paged(cache, page_table, layer, pe2, w1k, w1v, w2k, w2v, g_k):
    b, n_pages = page_table.shape
    pp = min(CMP_PAGES, n_pages)
    nc = n_pages // pp
    nblk = pp * BLOCKS_PER_PAGE
    dh = NSA_HEAD_DIM
    n_steps = b * nc

    def body(pt_ref, cache_ref, pe_ref, w1k_ref, w1v_ref, w2k_ref, w2v_ref, gk_ref, kc_ref, vc_ref, buf, sem):
        step = pl.program_id(0) * nc + pl.program_id(1)

        def page_copy(st, sl, k):
            jj, half = k // 2, k % 2
            page = pt_ref[st // nc, (st % nc) * pp + jj]
            return pltpu.make_async_copy(cache_ref.at[page, layer, :, pl.ds(half * dh, dh)],
                                         buf.at[sl, half, pl.ds(jj * PAGE_PITCH, PAGE_SIZE)], sem.at[sl])

        slot = _gather_pipeline(step, n_steps, page_copy, 2 * pp)

        def rows(half, r):
            per_q = [buf[slot, half, pl.ds(q * CMP_BLOCK + r, pp, stride=PAGE_PITCH), :] for q in range(BLOCKS_PER_PAGE)]
            return jnp.concatenate(per_q, axis=0) + pe_ref[r:r + 1, half * dh:(half + 1) * dh]

        acc_k = jnp.zeros((nblk, CMP_HIDDEN), F32)
        acc_v = jnp.zeros((nblk, CMP_HIDDEN), F32)
        for r in range(0, CMP_BLOCK, 2):
            xk = jnp.concatenate([rows(0, r), rows(0, r + 1)], axis=1)
            xv = jnp.concatenate([rows(1, r), rows(1, r + 1)], axis=1)
            acc_k = acc_k + _bdot(xk, w1k_ref[r * dh:(r + 2) * dh, :])
            acc_v = acc_v + _bdot(xv, w1v_ref[r * dh:(r + 2) * dh, :])
        kc, vc = _compress_tail(acc_k, acc_v, w2k_ref, w2v_ref, gk_ref)
        for q in range(BLOCKS_PER_PAGE):
            kc_ref[pl.ds(q, pp, stride=BLOCKS_PER_PAGE), :] = kc[q * pp:(q + 1) * pp, :]
            vc_ref[pl.ds(q, pp, stride=BLOCKS_PER_PAGE), :] = vc[q * pp:(q + 1) * pp, :]

    full = lambda a: pl.BlockSpec(a.shape, lambda i, c, pt: (0,) * a.ndim)
    gs = pltpu.PrefetchScalarGridSpec(
        num_scalar_prefetch=1, grid=(b, nc),
        in_specs=[pl.BlockSpec(memory_space=pl.ANY), full(pe2), full(w1k), full(w1v), full(w2k), full(w2v), full(g_k)],
        out_specs=[pl.BlockSpec((None, nblk, dh), lambda i, c, pt: (i, c, 0)),
                   pl.BlockSpec((None, nblk, dh), lambda i, c, pt: (i, c, 0))],
        scratch_shapes=[pltpu.VMEM((2, 2, pp * PAGE_PITCH, dh), F32), pltpu.SemaphoreType.DMA((2,))])
    shp = jax.ShapeDtypeStruct((b, n_pages * BLOCKS_PER_PAGE, dh), F32)
    return pl.pallas_call(body, grid_spec=gs, out_shape=[shp, shp],
                          compiler_params=_cparams(("arbitrary", "arbitrary")), name="nsa_compress_paged")(
        page_table, cache, pe2, w1k, w1v, w2k, w2v, g_k)


def _masked_softmax(l, mask):
    l = jnp.where(mask, l, -jnp.inf)
    mx = jnp.max(l, axis=-1, keepdims=True)
    mx = jnp.where(mx == -jnp.inf, 0.0, mx)
    p = jnp.where(mask, jnp.exp(l - mx), 0.0)
    return p / jnp.maximum(jnp.sum(p, axis=-1, keepdims=True), 1e-30)


def _select_blocks(imp, n_lanes, n_sel):
    lane = lax.broadcasted_iota(I32, imp.shape, 1)
    picked = jnp.zeros(imp.shape, F32)
    ids = []
    for _ in range(n_sel):
        mx = jnp.max(imp, axis=-1, keepdims=True)
        idx = jnp.min(jnp.where(imp == mx, lane, n_lanes), axis=-1, keepdims=True)
        hit = lane == idx
        picked = jnp.where(hit, 1.0, picked)
        imp = jnp.where(hit, -2.0, imp)
        ids.append(idx)
    return picked, ids


def _stack_heads(x, nh, dh):
    return jnp.concatenate([x[:, h * dh:(h + 1) * dh] for h in range(nh)], axis=0)


def _nsa_cmp_select(qn, kcmp, vcmp, pair, rel_bias):
    t = qn.shape[0]
    nc, ns = pair.shape
    nh, dh, tq = NSA_HEADS, NSA_HEAD_DIM, Q_TILE
    n_sel = min(N_SEL, ns)
    scale = dh ** -0.5

    def body(rb_ref, q_ref, kc_ref, vc_ref, pair_ref, o_ref, sel_ref):
        t0 = pl.program_id(0) * tq
        q4 = _stack_heads(q_ref[...], nh, dh)
        lc = (_bdot_nt(q4, kc_ref[...]) * scale).reshape(nh, tq, nc)
        tpos = t0 + lax.broadcasted_iota(I32, (tq, nc), 0)
        dist = tpos - (lax.broadcasted_iota(I32, (tq, nc), 1) * CMP_BLOCK + CMP_BLOCK - 1)
        lc = lc + jnp.stack([_t5_bias(dist, rb_ref, h) for h in range(nh)], axis=0)
        pc = _masked_softmax(lc, (dist >= 0)[None])
        oc = _bdot(pc.reshape(nh * tq, nc), vc_ref[...])
        for h in range(nh):
            o_ref[:, h * dh:(h + 1) * dh] = oc[h * tq:(h + 1) * tq, :]
        imp = jnp.dot(jnp.sum(pc, axis=0), pair_ref[...], precision=HI, preferred_element_type=F32)
        blk = lax.broadcasted_iota(I32, (tq, ns), 1)
        tq_pos = t0 + lax.broadcasted_iota(I32, (tq, ns), 0)
        cur = tq_pos // SEL_BLOCK
        forced = (blk == 0) | (blk == cur) | (blk == cur - 1)
        valid = blk * SEL_BLOCK <= tq_pos
        imp = jnp.where(forced, jnp.inf, jnp.where(valid, imp, -1.0))
        sel_ref[...] = _select_blocks(imp, ns, n_sel)[0]

    gs = pltpu.PrefetchScalarGridSpec(
        num_scalar_prefetch=0, grid=(t // tq,),
        in_specs=[pl.BlockSpec(memory_space=pltpu.SMEM),
                  pl.BlockSpec((tq, NSA_WIDTH), lambda i: (i, 0)), pl.BlockSpec((nc, dh), lambda i: (0, 0)),
                  pl.BlockSpec((nc, dh), lambda i: (0, 0)), pl.BlockSpec((nc, ns), lambda i: (0, 0))],
        out_specs=[pl.BlockSpec((tq, NSA_WIDTH), lambda i: (i, 0)), pl.BlockSpec((tq, ns), lambda i: (i, 0))])
    return pl.pallas_call(body, grid_spec=gs,
                          out_shape=[jax.ShapeDtypeStruct((t, NSA_WIDTH), F32), jax.ShapeDtypeStruct((t, ns), F32)],
                          compiler_params=_cparams(("arbitrary",)), name="nsa_cmp_select")(rel_bias, qn, kcmp, vcmp, pair)


def _n_bias_tiles(tq, tk):
    d = 0
    while d * tq - (tk - 1) < T5_THR[-1]:
        d += 1
    return d + 1


def _nsa_flash(qn, gates, o_cmp, selmask, sel_kv, win_kv, rel_bias):
    t = qn.shape[0]
    ns = selmask.shape[1]
    nh, dh, tq, tk = NSA_HEADS, NSA_HEAD_DIM, min(FLASH_Q_TILE, t), K_TILE
    n_bias = _n_bias_tiles(tq, tk)
    scale = dh ** -0.5
    spb = tk // SEL_BLOCK
    qi, kj = _causal_schedule(t // tq, tq, tk)

    def body(qi_ref, kj_ref, rb_ref, q_ref, g_ref, oc_ref, sm_ref, sk_ref, wk_ref, o_ref, bt_ref, qs_ref,
             ms, ls, accs, mw, lw, accw):
        i, j = qi_ref[pl.program_id(0)], kj_ref[pl.program_id(0)]
        t0, k0 = i * tq, j * tk
        rel = lax.broadcasted_iota(I32, (tq, tk), 0) - lax.broadcasted_iota(I32, (tq, tk), 1)

        @pl.when(pl.program_id(0) == 0)
        def _():
            for d in range(n_bias):
                for h in range(nh):
                    bt_ref[d, h] = _t5_bias(rel + d * tq, rb_ref, h)

        @pl.when(j == 0)
        def _():
            qs_ref[...] = (_stack_heads(q_ref[...], nh, dh) * scale).astype(BF16)
            for m_ref, l_ref, a_ref in ((ms, ls, accs), (mw, lw, accw)):
                m_ref[...] = jnp.full(m_ref.shape, -jnp.inf, F32)
                l_ref[...] = jnp.zeros(l_ref.shape, F32)
                a_ref[...] = jnp.zeros(a_ref.shape, F32)

        def update(m_ref, l_ref, a_ref, keys, bias, admit, vals):
            kb, vb = keys.astype(BF16), vals.astype(BF16)
            for h in range(nh):
                rows = pl.ds(h * tq, tq)
                s = _bdot_nt(qs_ref[rows, :], kb) + bias[h] + admit
                m_old = m_ref[rows, :]
                m_new = jnp.maximum(m_old, jnp.max(s, axis=-1, keepdims=True))
                m_safe = jnp.where(m_new == -jnp.inf, 0.0, m_new)
                alpha = jnp.exp(m_old - m_safe)
                pr = jnp.exp(s - m_safe)
                l_ref[rows, :] = l_ref[rows, :] * alpha + jnp.sum(pr, axis=-1, keepdims=True)
                a_ref[rows, :] = a_ref[rows, :] * alpha + _bdot(pr, vb)
                m_ref[rows, :] = m_new

        bias = bt_ref[jnp.minimum((t0 - k0) // tq, n_bias - 1)]
        dist = (t0 - k0) + rel
        expand = (lax.broadcasted_iota(I32, (ns, tk), 0) == j * spb + lax.broadcasted_iota(I32, (ns, tk), 1) // SEL_BLOCK)
        chosen = _bdot(sm_ref[...], expand.astype(F32)) > 0.5
        skv = sk_ref[...]
        update(ms, ls, accs, skv[:, 0:dh], bias, jnp.where(chosen & (dist >= 0), 0.0, -jnp.inf), skv[:, dh:KV_W])

        @pl.when(k0 + tk - 1 >= t0 - NSA_WINDOW)
        def _():
            wkv = wk_ref[...]
            update(mw, lw, accw, wkv[:, 0:dh], bias, jnp.where((dist >= 0) & (dist <= NSA_WINDOW), 0.0, -jnp.inf),
                   wkv[:, dh:KV_W])

        @pl.when(j == (t0 + tq - 1) // tk)
        def _():
            o_s = accs[...] / jnp.maximum(ls[...], 1e-30)
            o_w = accw[...] / jnp.maximum(lw[...], 1e-30)
            g = g_ref[...]
            for h in range(nh):
                rows = slice(h * tq, (h + 1) * tq)
                o_ref[:, h * dh:(h + 1) * dh] = (g[:, 3 * h:3 * h + 1] * oc_ref[:, h * dh:(h + 1) * dh]
                                                 + g[:, 3 * h + 1:3 * h + 2] * o_s[rows]
                                                 + g[:, 3 * h + 2:3 * h + 3] * o_w[rows])

    qrow = lambda w: pl.BlockSpec((tq, w), lambda s, qi, kj: (qi[s], 0))
    krow = pl.BlockSpec((tk, KV_W), lambda s, qi, kj: (kj[s], 0))
    gs = pltpu.PrefetchScalarGridSpec(
        num_scalar_prefetch=2, grid=(qi.shape[0],),
        in_specs=[pl.BlockSpec(memory_space=pltpu.SMEM), qrow(NSA_WIDTH), qrow(LANES), qrow(NSA_WIDTH), qrow(ns), krow, krow],
        out_specs=qrow(NSA_WIDTH),
        scratch_shapes=[pltpu.VMEM((n_bias, nh, tq, tk), F32), pltpu.VMEM((nh * tq, dh), BF16)]
        + [pltpu.VMEM((nh * tq, 1), F32), pltpu.VMEM((nh * tq, 1), F32), pltpu.VMEM((nh * tq, dh), F32)] * 2)
    return pl.pallas_call(body, grid_spec=gs, out_shape=jax.ShapeDtypeStruct((t, NSA_WIDTH), F32),
                          compiler_params=_cparams(("arbitrary",)), name="nsa_flash")(
        qi, kj, rel_bias, qn, gates, o_cmp, selmask, sel_kv, win_kv)


def _head_bias_rows(dist, rb_ref):
    n = dist.shape[1]
    return jnp.concatenate([_t5_bias(dist, rb_ref, h) for h in range(NSA_HEADS)]
                           + [jnp.zeros((8 - NSA_HEADS, n), F32)], axis=0)


def _nsa_cmp_select_decode(q8, kcmp, vcmp, pair, rel_bias, past):
    b, ncp, dh = kcmp.shape
    nsp = pair.shape[1]
    ns = (past + 1 + SEL_BLOCK - 1) // SEL_BLOCK
    ns_l = -(-ns // LANES) * LANES
    n_sel = min(N_SEL, ns)
    scale = dh ** -0.5
    sb = math.gcd(b, 8)
    assert past % PAGE_SIZE == 0 and ncp == past // CMP_BLOCK and nsp == past // SEL_BLOCK and ns_l > nsp

    def body(rb_ref, q_ref, kc_ref, vc_ref, pair_ref, o_ref, id_ref):
        dist = past - (lax.broadcasted_iota(I32, (1, ncp), 1) * CMP_BLOCK + CMP_BLOCK - 1)
        bias = _head_bias_rows(dist, rb_ref)
        mask = jnp.broadcast_to(dist >= 0, (8, ncp))
        head_sums = []
        for s in range(sb):
            pc = _masked_softmax(_bdot_nt(q_ref[s], kc_ref[s]) * scale + bias, mask)
            o_ref[s] = _bdot(pc, vc_ref[s])
            head_sums.append(jnp.sum(pc[0:NSA_HEADS], axis=0, keepdims=True))
        imp = jnp.dot(jnp.concatenate(head_sums, axis=0), pair_ref[...], precision=HI,
                      preferred_element_type=F32)
        imp = jnp.concatenate([imp, jnp.zeros((sb, ns_l - nsp), F32)], axis=1)
        blk = lax.broadcasted_iota(I32, (sb, ns_l), 1)
        cur = past // SEL_BLOCK
        forced = (blk == 0) | (blk == cur) | (blk == cur - 1)
        valid = blk * SEL_BLOCK <= past
        imp = jnp.where(blk < ns, jnp.where(forced, jnp.inf, jnp.where(valid, imp, -1.0)), -2.0)
        ids = _select_blocks(imp, ns_l, n_sel)[1]
        lane = lax.broadcasted_iota(I32, (sb, LANES), 1)
        out = jnp.zeros((sb, LANES), I32)
        for r, idx in enumerate(ids):
            out = jnp.where(lane == r, idx, out)
        id_ref[...] = out

    per = lambda r: pl.BlockSpec((sb, r, dh), lambda i: (i, 0, 0))
    gs = pltpu.PrefetchScalarGridSpec(
        num_scalar_prefetch=0, grid=(b // sb,),
        in_specs=[pl.BlockSpec(memory_space=pltpu.SMEM), per(8), per(ncp), per(ncp),
                  pl.BlockSpec((ncp, nsp), lambda i: (0, 0))],
        out_specs=[per(8), pl.BlockSpec((sb, LANES), lambda i: (i, 0))])
    return pl.pallas_call(body, grid_spec=gs,
                          out_shape=[jax.ShapeDtypeStruct((b, 8, dh), F32), jax.ShapeDtypeStruct((b, LANES), I32)],
                          compiler_params=_cparams(("arbitrary",)), name="nsa_cmp_select_decode")(
        rel_bias, q8, kcmp, vcmp, pair)


def _nsa_sel_win_decode(q8, gates, o_cmp, new_sel, new_win, sel_ids, page_table, cache_sel, win_state, rel_bias, layer, past,
                        earlier):
    b, n_pages = page_table.shape
    dh = NSA_HEAD_DIM
    n_sel = min(N_SEL, (past + 1 + SEL_BLOCK - 1) // SEL_BLOCK)
    n_cached = past // SEL_BLOCK
    halves = PAGE_SIZE // SEL_BLOCK
    wr = win_state.shape[2]
    keep = min(NSA_WINDOW, wr + 1)
    scale = dh ** -0.5
    assert wr == min(NSA_WINDOW, past) and keep == wr

    def body(ids_ref, pt_ref, rb_ref, q_ref, g_ref, oc_ref, ns_ref, nw_ref, ws_ref, cache_ref, *rest):
        early, (o_ref, nwin_ref, buf, sem) = rest[:len(earlier)], rest[len(earlier):]
        i = pl.program_id(0)
        if earlier:
            nwin_ref[0:layer] = early[0][...]

        def block_copy(st, sl, r):
            bid = jnp.minimum(ids_ref[st, r], n_cached - 1)
            return pltpu.make_async_copy(
                cache_ref.at[pt_ref[st, bid // halves], layer, pl.ds((bid % halves) * SEL_BLOCK, SEL_BLOCK)],
                buf.at[sl, r], sem.at[sl])

        slot = _gather_pipeline(i, b, block_copy, n_sel)
        q = q_ref[...]
        zero = jnp.zeros((1, 1), I32)

        def attend(parts, new_row, new_penalty):
            sn = jnp.sum(q * new_row[:, 0:dh], axis=-1, keepdims=True) * scale + _head_bias_rows(zero, rb_ref)
            sn = sn + new_penalty
            masked = [jnp.where(v, s, -jnp.inf) for s, v, _ in parts]
            mx = sn
            for s in masked:
                mx = jnp.maximum(mx, jnp.max(s, axis=-1, keepdims=True))
            mx = jnp.where(mx == -jnp.inf, 0.0, mx)
            pn = jnp.exp(sn - mx)
            den = pn
            num = pn * new_row[:, dh:KV_W]
            for s, (_, _, vals) in zip(masked, parts):
                pr = jnp.exp(s - mx)
                den = den + jnp.sum(pr, axis=-1, keepdims=True)
                num = num + _bdot(pr, vals)
            return num / jnp.maximum(den, 1e-30)

        lane_b = lax.broadcasted_iota(I32, (1, SEL_BLOCK), 1)
        parts = []
        n_cur = 0
        for r in range(n_sel):
            bid = ids_ref[i, r]
            kv = buf[slot, r]
            dist = past - (bid * SEL_BLOCK + lane_b) - jnp.where(bid < n_cached, 0, 2 * (past + SEL_BLOCK))
            s = _bdot_nt(q, kv[:, 0:dh]) * scale + _head_bias_rows(dist, rb_ref)
            parts.append((s, dist >= 0, kv[:, dh:KV_W]))
            n_cur = n_cur + jnp.where(bid == past // SEL_BLOCK, 1, 0)
        o_s = attend(parts, ns_ref[...], jnp.where(n_cur > 0, 0.0, -jnp.inf))

        wkv = ws_ref[...]
        dist_w = wr - lax.broadcasted_iota(I32, (1, wr), 1)
        sw = _bdot_nt(q, wkv[:, 0:dh]) * scale + _head_bias_rows(dist_w, rb_ref)
        o_w = attend([(sw, (dist_w >= 0) & (dist_w <= NSA_WINDOW), wkv[:, dh:KV_W])], nw_ref[...], 0.0)
        nwin_ref[layer, 0:keep - 1, :] = ws_ref[wr + 1 - keep:wr, :]
        nwin_ref[layer, keep - 1:keep, :] = nw_ref[...]

        g = jnp.broadcast_to(g_ref[...], (8, LANES))
        lane = lax.broadcasted_iota(I32, (8, LANES), 1)
        row = lax.broadcasted_iota(I32, (8, LANES), 0)
        gcol = [jnp.sum(jnp.where(lane == 3 * row + k, g, 0.0), axis=-1, keepdims=True) for k in range(3)]
        o_ref[...] = gcol[0] * oc_ref[...] + gcol[1] * o_s + gcol[2] * o_w

    per = lambda r, w: pl.BlockSpec((None, r, w), lambda i, ids, pt: (i, 0, 0))
    gs = pltpu.PrefetchScalarGridSpec(
        num_scalar_prefetch=2, grid=(b,),
        in_specs=[pl.BlockSpec(memory_space=pltpu.SMEM), per(8, dh), per(1, LANES), per(8, dh), per(1, KV_W), per(1, KV_W),
                  pl.BlockSpec((None, None, wr, KV_W), lambda i, ids, pt: (i, layer, 0, 0)),
                  pl.BlockSpec(memory_space=pl.ANY)]
        + [pl.BlockSpec((None, layer, keep, KV_W), lambda i, ids, pt: (i, 0, 0, 0)) for _ in earlier],
        out_specs=[per(8, dh), pl.BlockSpec((None, layer + 1, keep, KV_W), lambda i, ids, pt: (i, 0, 0, 0))],
        scratch_shapes=[pltpu.VMEM((2, n_sel, SEL_BLOCK, KV_W), F32), pltpu.SemaphoreType.DMA((2,))])
    return pl.pallas_call(body, grid_spec=gs,
                          out_shape=[jax.ShapeDtypeStruct((b, 8, dh), F32),
                                     jax.ShapeDtypeStruct((b, layer + 1, keep, KV_W), F32)],
                          compiler_params=_cparams(("arbitrary",)), name="nsa_sel_win_decode")(
        sel_ids, page_table, rel_bias, q8, gates, o_cmp, new_sel, new_win, win_state, cache_sel, *earlier)


def _pair_matrix(nc, ns):
    r = SEL_BLOCK // CMP_BLOCK
    return (jnp.arange(nc)[:, None] // r == jnp.arange(ns)[None, :]).astype(F32)


def _nsa_weights(q_norm, k_norm, pe, w1, w2):
    pe2 = jnp.concatenate([pe[0], pe[1]], axis=1)
    return (q_norm[None], k_norm[0][None], k_norm[1][None], k_norm[2][None], pe2,
            w1[0].astype(BF16), w1[1].astype(BF16), w2[0].astype(BF16), w2[1].astype(BF16))


def _nsa_prompt(p, nw, rel_bias, col_blk=0):
    g_q, g_kc, g_ks, g_kw, pe2, w1k, w1v, w2k, w2v = nw
    t = p.shape[0]
    qn, cmp_rows, sel_rows, win_rows, gates = _nsa_prep(p, g_q, g_ks, g_kw, min(t, 512), col_blk)
    nc, ns = t // CMP_BLOCK, t // SEL_BLOCK
    kcmp, vcmp = _nsa_compress(cmp_rows.reshape(nc, CMP_BLOCK * KV_W), pe2.reshape(1, CMP_BLOCK * KV_W),
                               w1k, w1v, w2k, w2v, g_kc)
    o_cmp, selmask = _nsa_cmp_select(qn, kcmp, vcmp, _pair_matrix(nc, ns), rel_bias)
    o = _nsa_flash(qn, gates, o_cmp, selmask, sel_rows, win_rows, rel_bias)
    return o, cmp_rows, sel_rows, win_rows


def _nsa_decode(p, cache_cmp, cache_sel, win_state, page_table, nw, rel_bias, layer, col_blk=0, earlier_win=()):
    g_q, g_kc, g_ks, g_kw, pe2, w1k, w1v, w2k, w2v = nw
    b, n_pages = page_table.shape
    past = n_pages * PAGE_SIZE
    qn, cmp_rows, sel_rows, win_rows, gates = _nsa_prep(p, g_q, g_ks, g_kw, b, col_blk)
    q8 = jnp.pad(qn.reshape(b, NSA_HEADS, NSA_HEAD_DIM), ((0, 0), (0, 8 - NSA_HEADS), (0, 0)))
    kcmp, vcmp = _nsa_compress_paged(cache_cmp, page_table, layer, pe2, w1k, w1v, w2k, w2v, g_kc)
    o_cmp, ids = _nsa_cmp_select_decode(q8, kcmp, vcmp, _pair_matrix(past // CMP_BLOCK, past // SEL_BLOCK), rel_bias, past)
    o8, new_win = _nsa_sel_win_decode(q8, gates[:, None, :], o_cmp, sel_rows[:, None, :], win_rows[:, None, :],
                                      ids[:, :N_SEL], page_table, cache_sel, win_state, rel_bias, layer, past,
                                      earlier_win)
    return o8[:, :NSA_HEADS].reshape(b, NSA_WIDTH), cmp_rows, sel_rows, new_win


def _moe(x, g, wr, br, w_gate, w_up, w_down, layer, tm_route, tm):
    h, route = _moe_router(x, g, wr, br, tm_route)
    eid = route[:, 0:2].astype(I32)
    wts = route[:, 2:4]
    row_src, row_w, slot_dest, tile_e, n_valid = _moe_dispatch(eid, wts, tm)
    ys = _moe_grouped(h, row_src, row_w[:, None], tile_e, n_valid, w_gate, w_up, w_down, layer, tm)
    return _moe_combine(x, ys, slot_dest, min(x.shape[0], 128))


P_COLS = 9 * MLA_P_COLS
NSA_COL_BLK = GDN_P_COLS // NSA_P_COLS
MLA_COL_BLK = P_COLS // MLA_P_COLS - 1
assert GDN_P_COLS % NSA_P_COLS == 0 and (NSA_COL_BLK + 1) * NSA_P_COLS <= MLA_COL_BLK * MLA_P_COLS


def _fused_w_in(w_in):
    n_gdn = 4 * GDN_WIDTH + 2 * GDN_HEADS
    n_nsa = NSA_WIDTH + 6 * NSA_HEAD_DIM + 3 * NSA_HEADS
    n_mla = MLA_Q_LORA + MLA_KV_LORA + MLA_ROPE
    assert w_in.shape[0] == n_gdn + n_nsa + n_mla
    rows = lambda a, b: w_in[a:b].astype(BF16)
    zeros = lambda n: jnp.zeros((n, w_in.shape[1]), BF16)
    o = MLA_COL_BLK * MLA_P_COLS
    return jnp.concatenate([rows(0, n_gdn), zeros(GDN_P_COLS - n_gdn), rows(n_gdn, n_gdn + n_nsa),
                            zeros(o - GDN_P_COLS - n_nsa), rows(n_gdn + n_nsa, n_gdn + n_nsa + n_mla),
                            zeros(P_COLS - o - n_mla)], axis=0)


def _layer_weights(l, norm1, w_in, gdn_conv, gdn_a_log, gdn_dt_bias, gdn_out_norm, nsa_q_norm, nsa_k_norm, nsa_cmp_pe,
                   nsa_cmp_w1, nsa_cmp_w2, mla_q_a_norm, mla_w_uq, mla_qk_norm, mla_kv_norm, mla_krope_norm, mla_w_uk,
                   mla_w_uv, w_out, norm2, moe_w_grp, moe_b_grp, moe_w_exp, moe_b_exp):
    lane8 = lambda v: jnp.zeros((1, LANES), F32).at[0, GDN_HEADS:2 * GDN_HEADS].set(v)
    wr = jnp.zeros((D_MODEL, LANES), F32).at[:, :N_GROUPS].set(moe_w_grp[l]).at[:, N_GROUPS:N_GROUPS + N_EXPERTS].set(moe_w_exp[l])
    br = jnp.zeros((1, LANES), F32).at[0, :N_GROUPS].set(moe_b_grp[l]).at[0, N_GROUPS:N_GROUPS + N_EXPERTS].set(moe_b_exp[l])
    return dict(
        norm1=norm1[l][None], w_in=_fused_w_in(jnp.transpose(w_in, (2, 0, 1))[:, l, :]),
        gdn=(gdn_conv[l], lane8(gdn_a_log[l]), lane8(gdn_dt_bias[l]), gdn_out_norm[l][None]),
        nsa=_nsa_weights(nsa_q_norm[l], nsa_k_norm[l], nsa_cmp_pe[l], nsa_cmp_w1[l], nsa_cmp_w2[l]),
        mla=_mla_weights(mla_q_a_norm[l], mla_w_uq[l], mla_qk_norm[l], mla_kv_norm[l], mla_krope_norm[l], mla_w_uk[l],
                         mla_w_uv[l]),
        w_out=w_out[l].astype(BF16), norm2=norm2[l][None], wr=wr, br=br)


def _prompt_layer(x, l, lw, rel_bias, moe_w):
    t = x.shape[0]
    p = _norm_matmul(x, lw["norm1"], lw["w_in"], 512, MLA_P_COLS)
    o_gdn, s_fin = _gdn_prompt(p, *lw["gdn"])
    new_conv = p[t - (CONV_W - 1):, 0:3 * GDN_WIDTH]
    o_nsa, cmp_rows, sel_rows, win_rows = _nsa_prompt(p, lw["nsa"], rel_bias, NSA_COL_BLK)
    cos, sin = _rope_tables(jnp.arange(t, dtype=I32))
    q, kv = _mla_prep(p, cos, sin, *lw["mla"][:-1], 512, MLA_COL_BLK)
    o_mla = _headwise_mm(_mla_flash(q, kv, 256, 512), lw["mla"][-1])
    x = _out_proj(x, o_gdn, o_nsa, o_mla, lw["w_out"], 512, 512)
    x = _moe(x, lw["norm2"], lw["wr"], lw["br"], *moe_w, l, 512, 256)
    return x, cmp_rows, sel_rows, kv, win_rows[t - min(NSA_WINDOW, t):], s_fin, new_conv


def _sample_layer(x, l, lw, rel_bias, moe_w, caches, states, page_table, earlier):
    b = x.shape[0]
    cache_cmp, cache_sel, cache_mla = caches
    win_state, gdn_state, conv_state = states
    past = page_table.shape[1] * PAGE_SIZE
    p = _norm_matmul(x, lw["norm1"], lw["w_in"], b, MLA_P_COLS)
    o_gdn, new_conv, new_gdn = _gdn_decode(p[:, None, :], conv_state, gdn_state, *lw["gdn"], l,
                                           (earlier[2], earlier[1]) if earlier else ())
    o_nsa, cmp_rows, sel_rows, new_win = _nsa_decode(p, cache_cmp, cache_sel, win_state, page_table, lw["nsa"], rel_bias,
                                                     l, NSA_COL_BLK, earlier[:1])
    cos, sin = _rope_tables(jnp.full((b,), past, I32))
    q, kv = _mla_prep(p, cos, sin, *lw["mla"][:-1], b, MLA_COL_BLK)
    q8 = jnp.pad(jnp.transpose(q, (1, 0, 2)), ((0, 0), (0, 8 - MLA_HEADS), (0, 0)))
    o_lat = _mla_decode(q8, kv[:, None, :], cache_mla, page_table, l)
    o_mla = _headwise_mm(jnp.transpose(o_lat[:, :MLA_HEADS], (1, 0, 2)), lw["mla"][-1])
    x = _out_proj(x, o_gdn[:, 0], o_nsa, o_mla, lw["w_out"], b, 512)
    x = _moe(x, lw["norm2"], lw["wr"], lw["br"], *moe_w, l, b, 32)
    return x, cmp_rows, sel_rows, kv, (new_win, new_gdn, new_conv)


def kernel(x_prompt, x_sample, cache_nsa_cmp, cache_nsa_sel, cache_mla, state_win_kv, state_gdn, state_conv, page_table, rel_bias, norm1, w_in, gdn_conv, gdn_a_log, gdn_dt_bias, gdn_out_norm, nsa_q_norm, nsa_k_norm, nsa_cmp_pe, nsa_cmp_w1, nsa_cmp_w2, mla_q_a_norm, mla_w_uq, mla_qk_norm, mla_kv_norm, mla_krope_norm, mla_w_uk, mla_w_uv, w_out, norm2, moe_w_grp, moe_b_grp, moe_w_exp, moe_b_exp, moe_w_gate, moe_w_up, moe_w_down):
    depth = norm1.shape[0]
    assert x_prompt.shape[0] == 1 and x_sample.shape[1] == 1
    moe_w = (moe_w_gate, moe_w_up, moe_w_down)
    lws = [_layer_weights(l, norm1, w_in, gdn_conv, gdn_a_log, gdn_dt_bias, gdn_out_norm, nsa_q_norm, nsa_k_norm,
                          nsa_cmp_pe, nsa_cmp_w1, nsa_cmp_w2, mla_q_a_norm, mla_w_uq, mla_qk_norm, mla_kv_norm,
                          mla_krope_norm, mla_w_uk, mla_w_uv, w_out, norm2, moe_w_grp, moe_b_grp, moe_w_exp, moe_b_exp)
           for l in range(depth)]

    xp = x_prompt[0]
    p_out = []
    for l in range(depth):
        xp, *rows = _prompt_layer(xp, l, lws[l], rel_bias, moe_w)
        p_out.append(rows)

    b = x_sample.shape[0]
    xs = x_sample[:, 0]
    conv_state = jnp.transpose(state_conv, (0, 2, 1, 3))
    cache_mla_t = jnp.transpose(cache_mla, (0, 1, 3, 2))
    s_out = []
    stacked = ()
    for l in range(depth):
        xs, *rows, stacked = _sample_layer(xs, l, lws[l], rel_bias, moe_w, (cache_nsa_cmp, cache_nsa_sel, cache_mla_t),
                                           (state_win_kv, state_gdn, conv_state), page_table, stacked)
        s_out.append(rows)
    s_win, s_gdn, s_conv = stacked

    def stack_p(k):
        return jnp.stack([p_out[l][k] for l in range(depth)], axis=0)[None]

    def stack_s(k, width):
        return jnp.stack([s_out[l][k] for l in range(depth)], axis=1).reshape(b, depth, 1, width)

    return (xp[None], xs[:, None],
            stack_p(0), stack_p(1), stack_p(2), stack_p(3), stack_p(4), stack_p(5),
            stack_s(0, KV_W), stack_s(1, KV_W), stack_s(2, MLA_LAT),
            s_win, s_gdn, jnp.transpose(s_conv, (0, 2, 1, 3)))
```

```python
import functools
import math

import jax
import jax.numpy as jnp
import numpy as np
from jax import lax
from jax.experimental import pallas as pl
from jax.experimental.pallas import tpu as pltpu

F32, BF16, I32 = jnp.float32, jnp.bfloat16, jnp.int32
HI = lax.Precision.HIGHEST

D_MODEL = 2048
PAGE_SIZE = 128
GDN_HEADS, GDN_DK, GDN_DV = 8, 128, 128
GDN_WIDTH = GDN_HEADS * GDN_DV
CONV_W = 4
GDN_CHUNK = 64
NSA_HEADS, NSA_HEAD_DIM = 4, 128
NSA_WIDTH = NSA_HEADS * NSA_HEAD_DIM
CMP_BLOCK, CMP_HIDDEN, SEL_BLOCK, N_SEL, NSA_WINDOW = 32, 256, 64, 16, 512
MLA_HEADS, MLA_Q_LORA, MLA_KV_LORA, MLA_NOPE, MLA_ROPE, MLA_V = 4, 512, 128, 128, 32, 128
MLA_QK = MLA_NOPE + MLA_ROPE
MLA_LAT = MLA_KV_LORA + MLA_ROPE
ROPE_THETA = 10000.0
N_BUCKETS, MAX_DISTANCE = 32, 128
N_GROUPS, EXPERTS_PER_GROUP = 8, 8
N_EXPERTS = N_GROUPS * EXPERTS_PER_GROUP
D_EXPERT = 256
RMS_EPS = 1e-6
L2_EPS = 1e-6
Q_TILE = 256
FLASH_Q_TILE = 128
K_TILE = 512
LANES = 128
VMEM_LIMIT = 56 << 20


def _cparams(sem, vmem=VMEM_LIMIT):
    return pltpu.CompilerParams(dimension_semantics=sem, vmem_limit_bytes=vmem)


def _bdot(a, b):
    return jnp.dot(a.astype(BF16), b.astype(BF16), preferred_element_type=F32)


def _bdot_nt(a, b):
    return lax.dot_general(a.astype(BF16), b.astype(BF16), (((1,), (1,)), ((), ())), preferred_element_type=F32)


def _einsum_bf16x3(eq, a, b):
    ah, bh = a.astype(BF16), b.astype(BF16)
    al, bl = (a - ah.astype(F32)).astype(BF16), (b - bh.astype(F32)).astype(BF16)
    dot = lambda x, y: jnp.einsum(eq, x, y, preferred_element_type=F32)
    return dot(ah, bh) + (dot(ah, bl) + dot(al, bh))


def _rms(x, g, n=None):
    n = x.shape[-1] if n is None else n
    return x * lax.rsqrt(jnp.sum(x * x, axis=-1, keepdims=True) / n + RMS_EPS) * g


def _sigmoid(x):
    return 1.0 / (1.0 + jnp.exp(-x))


def _t5_thresholds():
    n = np.arange(0, 4 * MAX_DISTANCE)
    exact = N_BUCKETS // 2
    out = []
    for dt in (np.float32, np.float64):
        nf = np.maximum(n, exact).astype(dt)
        large = exact + (np.log(nf / exact) / math.log(MAX_DISTANCE / exact) * (N_BUCKETS - exact)).astype(np.int32)
        b = np.where(n < exact, n, np.minimum(large, N_BUCKETS - 1))
        out.append([int(np.argmax(b >= k)) for k in range(1, N_BUCKETS)])
    assert out[0] == out[1]
    return out[0]


T5_THR = _t5_thresholds()


def _t5_bias(dist, tbl_ref, h):
    acc = jnp.full(dist.shape, tbl_ref[0, h], F32)
    for b in range(1, N_BUCKETS):
        acc = acc + jnp.where(dist >= T5_THR[b - 1], tbl_ref[b, h] - tbl_ref[b - 1, h], 0.0)
    return acc


def _causal_schedule(n_q, tq, tk):
    pairs = [(i, j) for i in range(n_q) for j in range((i * tq + tq - 1) // tk + 1)]
    return jnp.asarray([p[0] for p in pairs], I32), jnp.asarray([p[1] for p in pairs], I32)


def _gather_pipeline(step, n_steps, copies_of, n_copies, rolled=False):
    slot = step % 2

    def each(st, sl, op):
        if rolled:
            def one(k, carry):
                op(copies_of(st, sl, k))
                return carry
            lax.fori_loop(0, n_copies, one, 0, unroll=8)
        else:
            for k in range(n_copies):
                op(copies_of(st, sl, k))

    @pl.when(step == 0)
    def _():
        each(0, 0, lambda cp: cp.start())

    @pl.when(step + 1 < n_steps)
    def _():
        each(step + 1, 1 - slot, lambda cp: cp.start())

    @pl.when(step < n_steps)
    def _():
        each(step, slot, lambda cp: cp.wait())

    return slot


def _norm_matmul(x, g, w, tm, tn):
    m, d = x.shape
    n = w.shape[0]

    def body(x_ref, g_ref, w_ref, o_ref, h_ref):
        @pl.when(pl.program_id(1) == 0)
        def _():
            h_ref[...] = _rms(x_ref[...], g_ref[...]).astype(BF16)

        o_ref[...] = _bdot_nt(h_ref[...], w_ref[...])

    return pl.pallas_call(
        body, grid=(m // tm, n // tn),
        in_specs=[pl.BlockSpec((tm, d), lambda i, j: (i, 0)), pl.BlockSpec((1, d), lambda i, j: (0, 0)),
                  pl.BlockSpec((tn, d), lambda i, j: (j, 0))],
        out_specs=pl.BlockSpec((tm, tn), lambda i, j: (i, j)),
        out_shape=jax.ShapeDtypeStruct((m, n), F32),
        scratch_shapes=[pltpu.VMEM((tm, d), BF16)],
        compiler_params=_cparams(("arbitrary", "arbitrary")), name="norm_matmul")(x, g, w)


def _out_proj(x, a1, a2, a3, w, tm, tn):
    m, d = x.shape
    k1, k2, k3 = a1.shape[1], a2.shape[1], a3.shape[1]
    assert k1 % k2 == 0 and k2 == k3

    def body(x_ref, a1_ref, a2_ref, a3_ref, w1_ref, w2_ref, w3_ref, o_ref):
        o_ref[...] = (x_ref[...] + _bdot(a1_ref[...], w1_ref[...]) + _bdot(a2_ref[...], w2_ref[...])
                      + _bdot(a3_ref[...], w3_ref[...]))

    return pl.pallas_call(
        body, grid=(m // tm, d // tn),
        in_specs=[pl.BlockSpec((tm, tn), lambda i, j: (i, j)),
                  pl.BlockSpec((tm, k1), lambda i, j: (i, 0)), pl.BlockSpec((tm, k2), lambda i, j: (i, 0)),
                  pl.BlockSpec((tm, k3), lambda i, j: (i, 0)),
                  pl.BlockSpec((k1, tn), lambda i, j: (0, j)),
                  pl.BlockSpec((k2, tn), lambda i, j: (k1 // k2, j)),
                  pl.BlockSpec((k3, tn), lambda i, j: (k1 // k2 + 1, j))],
        out_specs=pl.BlockSpec((tm, tn), lambda i, j: (i, j)),
        out_shape=jax.ShapeDtypeStruct((m, d), F32),
        compiler_params=_cparams(("arbitrary", "arbitrary")), name="out_proj")(x, a1, a2, a3, w, w, w)


def _headwise_mm(x, w):
    h, m, k = x.shape
    n = w.shape[2]
    tm = min(m, 512)

    def body(x_ref, w_ref, o_ref):
        for i in range(h):
            o_ref[:, i * n:(i + 1) * n] = _bdot(x_ref[i], w_ref[i])

    return pl.pallas_call(
        body, grid=(m // tm,),
        in_specs=[pl.BlockSpec((h, tm, k), lambda i: (0, i, 0)), pl.BlockSpec((h, k, n), lambda i: (0, 0, 0))],
        out_specs=pl.BlockSpec((tm, h * n), lambda i: (i, 0)),
        out_shape=jax.ShapeDtypeStruct((m, h * n), F32),
        compiler_params=_cparams(("arbitrary",)), name="headwise_mm")(x, w)


def _moe_router(x, g, wr, br, tm):
    m, d = x.shape

    def body(x_ref, g_ref, wr_ref, br_ref, h_ref, r_ref):
        h = _rms(x_ref[...], g_ref[...])
        h_ref[...] = h
        lg = jnp.dot(h, wr_ref[...], precision=HI, preferred_element_type=F32) + br_ref[...]
        lane = lax.broadcasted_iota(I32, lg.shape, 1)
        is_g = lane < N_GROUPS
        lgm = jnp.where(is_g, lg, -jnp.inf)
        mg = jnp.max(lgm, axis=-1, keepdims=True)
        p_top = 1.0 / jnp.sum(jnp.where(is_g, jnp.exp(lgm - mg), 0.0), axis=-1, keepdims=True)
        gidx = jnp.min(jnp.where(lgm == mg, lane, LANES), axis=-1, keepdims=True)
        in_grp = (lane >= N_GROUPS) & (lane < N_GROUPS + N_EXPERTS) & (((lane - N_GROUPS) // EXPERTS_PER_GROUP) == gidx)
        le = jnp.where(in_grp, lg, -jnp.inf)
        m1 = jnp.max(le, axis=-1, keepdims=True)
        i1 = jnp.min(jnp.where(le == m1, lane, LANES), axis=-1, keepdims=True)
        le2 = jnp.where(lane == i1, -jnp.inf, le)
        m2 = jnp.max(le2, axis=-1, keepdims=True)
        i2 = jnp.min(jnp.where(le2 == m2, lane, LANES), axis=-1, keepdims=True)
        e2 = jnp.exp(m2 - m1)
        w1 = p_top / (1.0 + e2)
        w2 = p_top * e2 / (1.0 + e2)
        r_ref[...] = jnp.where(lane == 0, (i1 - N_GROUPS).astype(F32),
                               jnp.where(lane == 1, (i2 - N_GROUPS).astype(F32),
                                         jnp.where(lane == 2, w1, jnp.where(lane == 3, w2, 0.0))))

    return pl.pallas_call(
        body, grid=(m // tm,),
        in_specs=[pl.BlockSpec((tm, d), lambda i: (i, 0)), pl.BlockSpec((1, d), lambda i: (0, 0)),
                  pl.BlockSpec((d, LANES), lambda i: (0, 0)), pl.BlockSpec((1, LANES), lambda i: (0, 0))],
        out_specs=[pl.BlockSpec((tm, d), lambda i: (i, 0)), pl.BlockSpec((tm, LANES), lambda i: (i, 0))],
        out_shape=[jax.ShapeDtypeStruct((m, d), F32), jax.ShapeDtypeStruct((m, LANES), F32)],
        compiler_params=_cparams(("arbitrary",)), name="moe_router")(x, g, wr, br)


def _moe_dispatch(eid, wts, tm):
    t = eid.shape[0]
    n = 2 * t
    n_tiles = n // tm + N_EXPERTS
    e_flat = eid.reshape(n)
    order = jnp.argsort(e_flat, stable=True)
    e_s = e_flat[order]
    counts = jnp.sum(jax.nn.one_hot(e_flat, N_EXPERTS, dtype=I32), axis=0)
    tiles_e = (counts + tm - 1) // tm
    tile_end = jnp.cumsum(tiles_e)
    grp_start = jnp.cumsum(counts) - counts
    dest_sorted = (tile_end - tiles_e)[e_s] * tm + (jnp.arange(n, dtype=I32) - grp_start[e_s])
    row_src = jnp.zeros((n_tiles * tm,), I32).at[dest_sorted].set((order // 2).astype(I32))
    row_w = jnp.zeros((n_tiles * tm,), F32).at[dest_sorted].set(wts.reshape(n)[order])
    slot_dest = jnp.zeros((n,), I32).at[order].set(dest_sorted).reshape(t, 2)
    n_valid = tile_end[-1]
    ti = jnp.arange(n_tiles, dtype=I32)
    tile_e = jnp.minimum(jnp.searchsorted(tile_end, ti, side="right").astype(I32), N_EXPERTS - 1)
    tile_e = jnp.where(ti < n_valid, tile_e, tile_e[n_valid - 1])
    return row_src, row_w, slot_dest, tile_e, n_valid.reshape(1).astype(I32)


def _moe_grouped(h, row_src, ws, tile_e, n_valid, w_gate, w_up, w_down, layer, tm):
    d = h.shape[1]
    r = row_src.shape[0]
    f = w_gate.shape[-1]

    def body(te_ref, nv_ref, rs_ref, h_ref, w_ref, wg_ref, wu_ref, wd_ref, o_ref, xbuf, sem):
        def row_copy(st, sl, k):
            return pltpu.make_async_copy(h_ref.at[pl.ds(rs_ref[st * tm + k], 1)], xbuf.at[sl, pl.ds(k, 1)], sem.at[sl])

        slot = _gather_pipeline(pl.program_id(0), nv_ref[0], row_copy, tm, rolled=True)

        @pl.when(pl.program_id(0) < nv_ref[0])
        def _():
            x = xbuf[slot]
            a = _bdot(x, wg_ref[...])
            b = _bdot(x, wu_ref[...])
            act = a * _sigmoid(a) * b * w_ref[...]
            o_ref[...] = _bdot(act, wd_ref[...])

        @pl.when(pl.program_id(0) >= nv_ref[0])
        def _():
            o_ref[...] = jnp.zeros(o_ref.shape, F32)

    def wmap(i, te, nv, rs):
        return (layer, te[i], 0, 0)

    gs = pltpu.PrefetchScalarGridSpec(
        num_scalar_prefetch=3, grid=(r // tm,),
        in_specs=[pl.BlockSpec(memory_space=pl.ANY),
                  pl.BlockSpec((tm, 1), lambda i, te, nv, rs: (jnp.minimum(i, nv[0] - 1), 0)),
                  pl.BlockSpec((None, None, d, f), wmap), pl.BlockSpec((None, None, d, f), wmap),
                  pl.BlockSpec((None, None, f, d), wmap)],
        out_specs=pl.BlockSpec((tm, d), lambda i, te, nv, rs: (i, 0)),
        scratch_shapes=[pltpu.VMEM((2, tm, d), F32), pltpu.SemaphoreType.DMA((2,))])
    return pl.pallas_call(body, grid_spec=gs, out_shape=jax.ShapeDtypeStruct((r, d), F32),
                          compiler_params=_cparams(("arbitrary",)), name="moe_grouped")(
        tile_e, n_valid, row_src, h, ws, w_gate, w_up, w_down)


def _moe_combine(x, ys, slot_dest, tc):
    t, d = x.shape
    n_steps = t // tc
    dest = slot_dest.reshape(n_steps, tc, 2).transpose(0, 2, 1).reshape(-1)

    def body(d_ref, x_ref, ys_ref, o_ref, buf, sem):
        def row_copy(st, sl, k):
            return pltpu.make_async_copy(ys_ref.at[pl.ds(d_ref[st * 2 * tc + k], 1)], buf.at[sl, pl.ds(k, 1)], sem.at[sl])

        slot = _gather_pipeline(pl.program_id(0), n_steps, row_copy, 2 * tc, rolled=True)
        o_ref[...] = x_ref[...] + buf[slot, 0:tc, :] + buf[slot, tc:2 * tc, :]

    gs = pltpu.PrefetchScalarGridSpec(
        num_scalar_prefetch=1, grid=(n_steps,),
        in_specs=[pl.BlockSpec((tc, d), lambda i, dr: (i, 0)), pl.BlockSpec(memory_space=pl.ANY)],
        out_specs=pl.BlockSpec((tc, d), lambda i, dr: (i, 0)),
        scratch_shapes=[pltpu.VMEM((2, 2 * tc, d), F32), pltpu.SemaphoreType.DMA((2,))])
    return pl.pallas_call(body, grid_spec=gs, out_shape=jax.ShapeDtypeStruct((t, d), F32),
                          compiler_params=_cparams(("arbitrary",)), name="moe_combine")(dest, x, ys)


GDN_P_COLS = 4 * GDN_WIDTH + LANES


def _softplus(x):
    return jnp.maximum(x, 0.0) + jnp.log(1.0 + jnp.exp(-jnp.abs(x)))


def _gdn_gates(ba, al, dtb):
    return _sigmoid(ba), -jnp.exp(al) * _softplus(ba + dtb)


def _gdn_prompt(p, conv_w, a_log, dt_bias, out_norm):
    t = p.shape[0]
    c, nh, dk = GDN_CHUNK, GDN_HEADS, GDN_DK
    w3 = 3 * GDN_WIDTH

    def body(qkv_ref, z_ref, ba_ref, cw_ref, al_ref, dtb_ref, gn_ref, o_ref, sfin_ref, xbuf, s_ref):
        i = pl.program_id(0)

        @pl.when(i == 0)
        def _():
            xbuf[0:8, :] = jnp.zeros((8, w3), F32)
            s_ref[...] = jnp.zeros(s_ref.shape, F32)

        xbuf[8:8 + c, :] = qkv_ref[...]
        acc = cw_ref[0:1, :] * xbuf[5:5 + c, :]
        for j in range(1, CONV_W):
            acc = acc + cw_ref[j:j + 1, :] * xbuf[5 + j:5 + j + c, :]
        xbuf[0:8, :] = xbuf[c:c + 8, :]
        qkv = acc * _sigmoid(acc)

        def heads(off):
            return jnp.stack([qkv[:, off + h * dk: off + (h + 1) * dk] for h in range(nh)], axis=0)

        q, k, v = heads(0), heads(GDN_WIDTH), heads(2 * GDN_WIDTH)
        q = q * lax.rsqrt(jnp.sum(q * q, axis=-1, keepdims=True) + L2_EPS) * (dk ** -0.5)
        k = k * lax.rsqrt(jnp.sum(k * k, axis=-1, keepdims=True) + L2_EPS)
        beta_l, g_l = _gdn_gates(ba_ref[...], al_ref[...], dtb_ref[...])
        row = lax.broadcasted_iota(I32, (c, c), 0)
        col = lax.broadcasted_iota(I32, (c, c), 1)
        tril = (row >= col).astype(F32)
        gc_l = jnp.dot(tril, g_l, precision=HI, preferred_element_type=F32)
        beta = jnp.stack([beta_l[:, h:h + 1] for h in range(nh)], axis=0)
        gcol = jnp.stack([gc_l[:, nh + h:nh + h + 1] for h in range(nh)], axis=0)
        eye = (row == col)[None]
        grow = jnp.sum(jnp.where(eye, gcol, 0.0), axis=1, keepdims=True)
        glast = gcol[:, c - 1:c, :]
        causal = (row >= col)[None]
        strict = (row > col)[None]
        decay = jnp.exp(jnp.where(causal, gcol - grow, -jnp.inf))
        eg = jnp.exp(gcol)
        kb = k * beta
        kk = jnp.einsum("hid,hjd->hij", kb.astype(BF16), k.astype(BF16), preferred_element_type=F32)
        nmat = -jnp.where(strict, kk * decay, 0.0)
        tmat = jnp.where(eye, 1.0, 0.0) + nmat
        pw = nmat
        for _ in range(5):
            pw = _einsum_bf16x3("hij,hjk->hik", pw, pw)
            tmat = tmat + _einsum_bf16x3("hij,hjk->hik", tmat, pw)
        tb = tmat.astype(BF16)
        u = jnp.einsum("hij,hjd->hid", tb, (v * beta).astype(BF16), preferred_element_type=F32)
        w = jnp.einsum("hij,hjd->hid", tb, (kb * eg).astype(BF16), preferred_element_type=F32)
        attn = jnp.einsum("hid,hjd->hij", q.astype(BF16), k.astype(BF16), preferred_element_type=F32) * decay
        qg = q * eg
        kd = k * jnp.exp(glast - gcol)
        s = s_ref[...]
        sb = s.astype(BF16)
        v_new = u - jnp.einsum("hik,hkd->hid", w.astype(BF16), sb, preferred_element_type=F32)
        vb = v_new.astype(BF16)
        o = (jnp.einsum("hik,hkd->hid", qg.astype(BF16), sb, preferred_element_type=F32)
             + jnp.einsum("hij,hjd->hid", attn.astype(BF16), vb, preferred_element_type=F32))
        s_new = s * jnp.exp(glast) + jnp.einsum("hik,hid->hkd", kd.astype(BF16), vb, preferred_element_type=F32)
        s_ref[...] = s_new
        sfin_ref[...] = s_new
        z = z_ref[...]
        for h in range(nh):
            zh = z[:, h * dk:(h + 1) * dk]
            o_ref[:, h * dk:(h + 1) * dk] = _rms(o[h], gn_ref[...]) * (zh * _sigmoid(zh))

    return pl.pallas_call(
        body, grid=(t // c,),
        in_specs=[pl.BlockSpec((c, w3), lambda i: (i, 0)), pl.BlockSpec((c, GDN_WIDTH), lambda i: (i, 3)),
                  pl.BlockSpec((c, LANES), lambda i: (i, 4 * GDN_WIDTH // LANES)),
                  pl.BlockSpec((CONV_W, w3), lambda i: (0, 0)),
                  pl.BlockSpec((1, LANES), lambda i: (0, 0)), pl.BlockSpec((1, LANES), lambda i: (0, 0)),
                  pl.BlockSpec((1, GDN_DV), lambda i: (0, 0))],
        out_specs=[pl.BlockSpec((c, GDN_WIDTH), lambda i: (i, 0)),
                   pl.BlockSpec((nh, dk, GDN_DV), lambda i: (0, 0, 0))],
        out_shape=[jax.ShapeDtypeStruct((t, GDN_WIDTH), F32), jax.ShapeDtypeStruct((nh, dk, GDN_DV), F32)],
        scratch_shapes=[pltpu.VMEM((c + 8, w3), F32), pltpu.VMEM((nh, dk, GDN_DV), F32)],
        compiler_params=_cparams(("arbitrary",)), name="gdn_prompt")(p, p, p, conv_w, a_log, dt_bias, out_norm)


def _gdn_decode(p3, conv_state, state, conv_w, a_log, dt_bias, out_norm, layer, earlier):
    b = p3.shape[0]
    nh, dk = GDN_HEADS, GDN_DK
    w3 = 3 * GDN_WIDTH
    ne = layer if earlier else 0
    assert ne == layer

    def body(qkv_ref, z_ref, ba_ref, cs_ref, st_ref, cw_ref, al_ref, dtb_ref, gn_ref, *rest):
        early, (o_ref, ncs_ref, nst_ref) = rest[:len(earlier)], rest[len(earlier):]
        if earlier:
            ncs_ref[:, 0:ne, :] = early[0][...]
            nst_ref[0:ne] = early[1][...]
        x = qkv_ref[...]
        rows = [cs_ref[j, layer:layer + 1, :] for j in range(CONV_W - 1)] + [x]
        acc = cw_ref[0:1, :] * rows[0]
        for j in range(1, CONV_W):
            acc = acc + cw_ref[j:j + 1, :] * rows[j]
        for j in range(CONV_W - 1):
            ncs_ref[j, ne:ne + 1, :] = rows[j + 1]
        qkv = acc * _sigmoid(acc)

        def heads(off):
            return jnp.concatenate([qkv[:, off + h * dk: off + (h + 1) * dk] for h in range(nh)], axis=0)

        q, k, v = heads(0), heads(GDN_WIDTH), heads(2 * GDN_WIDTH)
        q = q * lax.rsqrt(jnp.sum(q * q, axis=-1, keepdims=True) + L2_EPS) * (dk ** -0.5)
        k = k * lax.rsqrt(jnp.sum(k * k, axis=-1, keepdims=True) + L2_EPS)
        qk = jnp.sum(q * k, axis=-1, keepdims=True)
        beta_l, g_l = _gdn_gates(ba_ref[...], al_ref[...], dtb_ref[...])
        kt = k.T
        qt = q.T
        z = z_ref[...]
        for h in range(nh):
            s = st_ref[h]
            bh = beta_l[:, h:h + 1]
            egh = jnp.exp(g_l[:, nh + h:nh + h + 1])
            kc = kt[:, h:h + 1]
            qc = qt[:, h:h + 1]
            ks = jnp.sum(kc * s, axis=0, keepdims=True)
            qs = jnp.sum(qc * s, axis=0, keepdims=True)
            v_new = v[h:h + 1, :] * bh - ks * (bh * egh)
            o = qs * egh + qk[h:h + 1, :] * v_new
            nst_ref[ne, h] = s * egh + kc * v_new
            zh = z[:, h * dk:(h + 1) * dk]
            o_ref[:, h * dk:(h + 1) * dk] = _rms(o, gn_ref[...]) * (zh * _sigmoid(zh))

    per_sample = lambda shape: pl.BlockSpec((None,) + shape, lambda i: (i,) + (0,) * len(shape))
    conv_out, state_out = (CONV_W - 1, ne + 1, w3), (ne + 1, nh, dk, GDN_DV)
    return pl.pallas_call(
        body, grid=(b,),
        in_specs=[pl.BlockSpec((None, 1, w3), lambda i: (i, 0, 0)),
                  pl.BlockSpec((None, 1, GDN_WIDTH), lambda i: (i, 0, 3)),
                  pl.BlockSpec((None, 1, LANES), lambda i: (i, 0, 4 * GDN_WIDTH // LANES)),
                  per_sample(conv_state.shape[1:]),
                  pl.BlockSpec((None, None, nh, dk, GDN_DV), lambda i: (i, layer, 0, 0, 0)),
                  pl.BlockSpec((CONV_W, w3), lambda i: (0, 0)),
                  pl.BlockSpec((1, LANES), lambda i: (0, 0)), pl.BlockSpec((1, LANES), lambda i: (0, 0)),
                  pl.BlockSpec((1, GDN_DV), lambda i: (0, 0))] + [per_sample(a.shape[1:]) for a in earlier],
        out_specs=[pl.BlockSpec((None, 1, GDN_WIDTH), lambda i: (i, 0, 0)), per_sample(conv_out), per_sample(state_out)],
        out_shape=[jax.ShapeDtypeStruct((b, 1, GDN_WIDTH), F32), jax.ShapeDtypeStruct((b,) + conv_out, F32),
                   jax.ShapeDtypeStruct((b,) + state_out, F32)],
        compiler_params=_cparams(("arbitrary",)), name="gdn_decode")(
        p3, p3, p3, conv_state, state, conv_w, a_log, dt_bias, out_norm, *earlier)


MLA_P_COLS = MLA_Q_LORA + 2 * LANES


def _rope_tables(pos):
    half = MLA_ROPE // 2
    inv = ROPE_THETA ** (-jnp.arange(half, dtype=F32) / half)
    ang = jnp.tile(pos.astype(F32)[:, None] * inv, (1, LANES // half))
    sign = jnp.where((jnp.arange(LANES) % MLA_ROPE) < half, -1.0, 1.0).astype(F32)
    return jnp.cos(ang), jnp.sin(ang) * sign


def _mla_weights(q_a_norm, w_uq, qk_norm, kv_norm, krope_norm, w_uk, w_uv):
    nh = MLA_HEADS
    w4 = w_uq.reshape(MLA_Q_LORA, nh, MLA_QK)
    w_uq_p = jnp.concatenate([w4[:, :, :MLA_NOPE].reshape(MLA_Q_LORA, nh * MLA_NOPE),
                              w4[:, :, MLA_NOPE:].reshape(MLA_Q_LORA, nh * MLA_ROPE)], axis=1).astype(BF16)
    g_nope = qk_norm[None, :MLA_NOPE]
    g_rope = jnp.tile(qk_norm[MLA_NOPE:], nh)[None]
    w_uk_t = jnp.transpose(w_uk, (1, 2, 0)).astype(BF16)
    w_uv_t = jnp.transpose(w_uv, (1, 0, 2)).astype(BF16)
    g_kr = jnp.pad(krope_norm, (0, LANES - MLA_ROPE))[None]
    return q_a_norm[None], w_uq_p, g_nope, g_rope, w_uk_t, kv_norm[None], g_kr, w_uv_t


def _rope_rotate(x, cos, sin_signed):
    lane = lax.broadcasted_iota(I32, x.shape, 1)
    half = MLA_ROPE // 2
    swapped = jnp.where(lane % MLA_ROPE < half, pltpu.roll(x, LANES - half, 1), pltpu.roll(x, half, 1))
    return x * cos + swapped * sin_signed


def _mla_prep(p, cos, sin, g_qa, w_uq, g_nope, g_rope, w_uk, g_kv, g_kr, tm, col_blk=0):
    m = p.shape[0]
    nh = MLA_HEADS

    def body(p_ref, cos_ref, sin_ref, gqa_ref, wuq_ref, gn_ref, gr_ref, wuk_ref, gkv_ref, gkr_ref, q_ref, kv_ref):
        cos, sin = cos_ref[...], sin_ref[...]
        cq = _rms(p_ref[:, 0:MLA_Q_LORA], gqa_ref[...])
        qf = _bdot(cq, wuq_ref[...])
        rope_all = qf[:, nh * MLA_NOPE:]
        r2 = rope_all * rope_all
        lane = lax.broadcasted_iota(I32, rope_all.shape, 1)
        inv_lane = jnp.zeros(rope_all.shape, F32)
        for h in range(nh):
            nope = qf[:, h * MLA_NOPE:(h + 1) * MLA_NOPE]
            in_h = lane // MLA_ROPE == h
            ss = jnp.sum(nope * nope, axis=-1, keepdims=True) + jnp.sum(jnp.where(in_h, r2, 0.0), axis=-1, keepdims=True)
            inv = lax.rsqrt(ss / MLA_QK + RMS_EPS)
            inv_lane = jnp.where(in_h, inv, inv_lane)
            q_ref[h, :, 0:MLA_KV_LORA] = _bdot(nope * inv * gn_ref[...], wuk_ref[h])
        rot = _rope_rotate(rope_all * inv_lane * gr_ref[...], cos, sin)
        for h in range(nh):
            q_ref[h, :, MLA_KV_LORA:MLA_LAT] = rot[:, h * MLA_ROPE:(h + 1) * MLA_ROPE]
        kv_ref[:, 0:MLA_KV_LORA] = _rms(p_ref[:, MLA_Q_LORA:MLA_Q_LORA + MLA_KV_LORA], gkv_ref[...])
        kr = p_ref[:, MLA_Q_LORA + MLA_KV_LORA:MLA_P_COLS]
        kv_ref[:, MLA_KV_LORA:MLA_LAT] = _rope_rotate(_rms(kr, gkr_ref[...], MLA_ROPE), cos, sin)[:, 0:MLA_ROPE]

    full = lambda a: pl.BlockSpec(a.shape, lambda i: (0,) * a.ndim)
    return pl.pallas_call(
        body, grid=(m // tm,),
        in_specs=[pl.BlockSpec((tm, MLA_P_COLS), lambda i: (i, col_blk)), pl.BlockSpec((tm, LANES), lambda i: (i, 0)),
                  pl.BlockSpec((tm, LANES), lambda i: (i, 0)), full(g_qa), full(w_uq), full(g_nope), full(g_rope),
                  full(w_uk), full(g_kv), full(g_kr)],
        out_specs=[pl.BlockSpec((nh, tm, MLA_LAT), lambda i: (0, i, 0)), pl.BlockSpec((tm, MLA_LAT), lambda i: (i, 0))],
        out_shape=[jax.ShapeDtypeStruct((nh, m, MLA_LAT), F32), jax.ShapeDtypeStruct((m, MLA_LAT), F32)],
        compiler_params=_cparams(("arbitrary",)), name="mla_prep")(
        p, cos, sin, g_qa, w_uq, g_nope, g_rope, w_uk, g_kv, g_kr)


def _mla_flash(q, kv, tq, tk):
    nh, t, _ = q.shape
    scale = MLA_QK ** -0.5
    qi, kj = _causal_schedule(t // tq, tq, tk)

    def body(qi_ref, kj_ref, q_ref, kv_ref, o_ref, qs_ref, m_ref, l_ref, acc_ref):
        i, j = qi_ref[pl.program_id(0)], kj_ref[pl.program_id(0)]

        @pl.when(j == 0)
        def _():
            qs_ref[...] = (q_ref[...].reshape(nh * tq, MLA_LAT) * scale).astype(BF16)
            m_ref[...] = jnp.full(m_ref.shape, -jnp.inf, F32)
            l_ref[...] = jnp.zeros(l_ref.shape, F32)
            acc_ref[...] = jnp.zeros(acc_ref.shape, F32)

        kb = kv_ref[...].astype(BF16)
        rel = (i * tq - j * tk) + lax.broadcasted_iota(I32, (tq, tk), 0) - lax.broadcasted_iota(I32, (tq, tk), 1)
        causal = jnp.where(rel >= 0, 0.0, -jnp.inf)
        s = (_bdot_nt(qs_ref[...], kb).reshape(nh, tq, tk) + causal[None]).reshape(nh * tq, tk)
        m_new = jnp.maximum(m_ref[...], jnp.max(s, axis=-1, keepdims=True))
        alpha = jnp.exp(m_ref[...] - m_new)
        pr = jnp.exp(s - m_new)
        l_ref[...] = l_ref[...] * alpha + jnp.sum(pr, axis=-1, keepdims=True)
        acc_ref[...] = acc_ref[...] * alpha + jnp.dot(pr.astype(BF16), kb[:, 0:MLA_KV_LORA], preferred_element_type=F32)
        m_ref[...] = m_new

        @pl.when(j == (i * tq + tq - 1) // tk)
        def _():
            o_ref[...] = (acc_ref[...] / l_ref[...]).reshape(nh, tq, MLA_KV_LORA)

    gs = pltpu.PrefetchScalarGridSpec(
        num_scalar_prefetch=2, grid=(qi.shape[0],),
        in_specs=[pl.BlockSpec((nh, tq, MLA_LAT), lambda s, qi, kj: (0, qi[s], 0)),
                  pl.BlockSpec((tk, MLA_LAT), lambda s, qi, kj: (kj[s], 0))],
        out_specs=pl.BlockSpec((nh, tq, MLA_KV_LORA), lambda s, qi, kj: (0, qi[s], 0)),
        scratch_shapes=[pltpu.VMEM((nh * tq, MLA_LAT), BF16), pltpu.VMEM((nh * tq, 1), F32),
                        pltpu.VMEM((nh * tq, 1), F32), pltpu.VMEM((nh * tq, MLA_KV_LORA), F32)])
    return pl.pallas_call(body, grid_spec=gs, out_shape=jax.ShapeDtypeStruct((nh, t, MLA_KV_LORA), F32),
                          compiler_params=_cparams(("arbitrary",)), name="mla_flash")(qi, kj, q, kv)


PAGES_PER_STEP = 64


def _mla_decode(q8, knew, cache, page_table, layer):
    b, n_pages = page_table.shape
    pp = min(PAGES_PER_STEP, n_pages)
    nc = n_pages // pp
    scale = MLA_QK ** -0.5

    def body(pt_ref, q_ref, kn_ref, cache_ref, o_ref, buf, sem, m_ref, l_ref, acc_ref):
        c = pl.program_id(1)
        step = pl.program_id(0) * nc + c

        def page_copy(st, sl, jj):
            page = pt_ref[st // nc, (st % nc) * pp + jj]
            return pltpu.make_async_copy(cache_ref.at[page, layer], buf.at[sl, jj], sem.at[sl])

        slot = _gather_pipeline(step, b * nc, page_copy, pp)

        @pl.when(c == 0)
        def _():
            m_ref[...] = jnp.full(m_ref.shape, -jnp.inf, F32)
            l_ref[...] = jnp.zeros(l_ref.shape, F32)
            acc_ref[...] = jnp.zeros(acc_ref.shape, F32)

        qb = q_ref[...].astype(BF16)
        kbs = [buf[slot, jj].astype(BF16) for jj in range(pp)]
        s = jnp.concatenate([jnp.dot(qb, kb, preferred_element_type=F32) for kb in kbs], axis=1) * scale
        m_new = jnp.maximum(m_ref[...], jnp.max(s, axis=-1, keepdims=True))
        alpha = jnp.exp(m_ref[...] - m_new)
        pr = jnp.exp(s - m_new)
        l_ref[...] = l_ref[...] * alpha + jnp.sum(pr, axis=-1, keepdims=True)
        pv = acc_ref[...] * alpha
        for jj, kb in enumerate(kbs):
            pv = pv + _bdot_nt(pr[:, jj * PAGE_SIZE:(jj + 1) * PAGE_SIZE], kb[0:MLA_KV_LORA, :])
        acc_ref[...] = pv
        m_ref[...] = m_new

        @pl.when(c == nc - 1)
        def _():
            kn = kn_ref[...]
            sn = jnp.sum(q_ref[...] * kn, axis=-1, keepdims=True) * scale
            m2 = jnp.maximum(m_ref[...], sn)
            a2 = jnp.exp(m_ref[...] - m2)
            pn = jnp.exp(sn - m2)
            l2 = l_ref[...] * a2 + pn
            o_ref[...] = (acc_ref[...] * a2 + pn * kn[:, 0:MLA_KV_LORA]) / l2

    gs = pltpu.PrefetchScalarGridSpec(
        num_scalar_prefetch=1, grid=(b, nc),
        in_specs=[pl.BlockSpec((None, 8, MLA_LAT), lambda i, c, pt: (i, 0, 0)),
                  pl.BlockSpec((None, 1, MLA_LAT), lambda i, c, pt: (i, 0, 0)), pl.BlockSpec(memory_space=pl.ANY)],
        out_specs=pl.BlockSpec((None, 8, MLA_KV_LORA), lambda i, c, pt: (i, 0, 0)),
        scratch_shapes=[pltpu.VMEM((2, pp, MLA_LAT, PAGE_SIZE), F32), pltpu.SemaphoreType.DMA((2,)),
                        pltpu.VMEM((8, 1), F32), pltpu.VMEM((8, 1), F32), pltpu.VMEM((8, MLA_KV_LORA), F32)])
    return pl.pallas_call(body, grid_spec=gs, out_shape=jax.ShapeDtypeStruct((b, 8, MLA_KV_LORA), F32),
                          compiler_params=_cparams(("arbitrary", "arbitrary")), name="mla_decode")(
        page_table, q8, knew, cache)


NSA_P_COLS = NSA_WIDTH + 7 * LANES
KV_W = 2 * NSA_HEAD_DIM


def _nsa_prep(p, g_q, g_ksel, g_kwin, tm, col_blk=0):
    m = p.shape[0]
    dh = NSA_HEAD_DIM

    def body(p_ref, gq_ref, gs_ref, gw_ref, q_ref, c_ref, s_ref, w_ref, g_ref):
        for h in range(NSA_HEADS):
            q_ref[:, h * dh:(h + 1) * dh] = _rms(p_ref[:, h * dh:(h + 1) * dh], gq_ref[...])
        o = NSA_WIDTH
        c_ref[...] = p_ref[:, o:o + 2 * dh]
        s_ref[:, 0:dh] = _rms(p_ref[:, o + 2 * dh:o + 3 * dh], gs_ref[...])
        s_ref[:, dh:2 * dh] = p_ref[:, o + 3 * dh:o + 4 * dh]
        w_ref[:, 0:dh] = _rms(p_ref[:, o + 4 * dh:o + 5 * dh], gw_ref[...])
        w_ref[:, dh:2 * dh] = p_ref[:, o + 5 * dh:o + 6 * dh]
        g_ref[...] = _sigmoid(p_ref[:, o + 6 * dh:o + 7 * dh])

    row = lambda w: pl.BlockSpec((tm, w), lambda i: (i, 0))
    vec = pl.BlockSpec((1, dh), lambda i: (0, 0))
    return pl.pallas_call(
        body, grid=(m // tm,), in_specs=[pl.BlockSpec((tm, NSA_P_COLS), lambda i: (i, col_blk)), vec, vec, vec],
        out_specs=[row(NSA_WIDTH), row(KV_W), row(KV_W), row(KV_W), row(LANES)],
        out_shape=[jax.ShapeDtypeStruct((m, w), F32) for w in (NSA_WIDTH, KV_W, KV_W, KV_W, LANES)],
        compiler_params=_cparams(("arbitrary",)), name="nsa_prep")(p, g_q, g_ksel, g_kwin)


def _gelu_tanh(x):
    return 0.5 * x * (1.0 + jnp.tanh(math.sqrt(2.0 / math.pi) * (x + 0.044715 * x * x * x)))


def _compress_tail(acc_k, acc_v, w2k_ref, w2v_ref, gk_ref):
    kc = _rms(_bdot(_gelu_tanh(acc_k), w2k_ref[...]), gk_ref[...])
    vc = _bdot(_gelu_tanh(acc_v), w2v_ref[...])
    return kc, vc


def _nsa_compress(xb, pe, w1k, w1v, w2k, w2v, g_k):
    nb = xb.shape[0]
    bm = min(nb, 256)
    dh = NSA_HEAD_DIM

    def body(x_ref, pe_ref, w1k_ref, w1v_ref, w2k_ref, w2v_ref, gk_ref, kc_ref, vc_ref):
        x = x_ref[...] + pe_ref[...]
        xk = jnp.concatenate([x[:, r * KV_W:r * KV_W + dh] for r in range(CMP_BLOCK)], axis=1)
        xv = jnp.concatenate([x[:, r * KV_W + dh:(r + 1) * KV_W] for r in range(CMP_BLOCK)], axis=1)
        kc, vc = _compress_tail(_bdot(xk, w1k_ref[...]), _bdot(xv, w1v_ref[...]), w2k_ref, w2v_ref, gk_ref)
        kc_ref[...] = kc
        vc_ref[...] = vc

    full = lambda a: pl.BlockSpec(a.shape, lambda i: (0,) * a.ndim)
    return pl.pallas_call(
        body, grid=(nb // bm,),
        in_specs=[pl.BlockSpec((bm, CMP_BLOCK * KV_W), lambda i: (i, 0)), full(pe), full(w1k), full(w1v), full(w2k),
                  full(w2v), full(g_k)],
        out_specs=[pl.BlockSpec((bm, dh), lambda i: (i, 0)), pl.BlockSpec((bm, dh), lambda i: (i, 0))],
        out_shape=[jax.ShapeDtypeStruct((nb, dh), F32), jax.ShapeDtypeStruct((nb, dh), F32)],
        compiler_params=_cparams(("arbitrary",)), name="nsa_compress")(xb, pe, w1k, w1v, w2k, w2v, g_k)


CMP_PAGES = 64
BLOCKS_PER_PAGE = PAGE_SIZE // CMP_BLOCK
PAGE_PITCH = PAGE_SIZE + 8


def _nsa_compress_paged(cache, page_table, layer, pe2, w1k, w1v, w2k, w2v, g_k):
    b, n_pages = page_table.shape
    pp = min(CMP_PAGES, n_pages)
    nc = n_pages // pp
    nblk = pp * BLOCKS_PER_PAGE
    dh = NSA_HEAD_DIM
    n_steps = b * nc

    def body(pt_ref, cache_ref, pe_ref, w1k_ref, w1v_ref, w2k_ref, w2v_ref, gk_ref, kc_ref, vc_ref, buf, sem):
        step = pl.program_id(0) * nc + pl.program_id(1)

        def page_copy(st, sl, k):
            jj, half = k // 2, k % 2
            page = pt_ref[st // nc, (st % nc) * pp + jj]
            return pltpu.make_async_copy(cache_ref.at[page, layer, :, pl.ds(half * dh, dh)],
                                         buf.at[sl, half, pl.ds(jj * PAGE_PITCH, PAGE_SIZE)], sem.at[sl])

        slot = _gather_pipeline(step, n_steps, page_copy, 2 * pp)

        def rows(half, r):
            per_q = [buf[slot, half, pl.ds(q * CMP_BLOCK + r, pp, stride=PAGE_PITCH), :] for q in range(BLOCKS_PER_PAGE)]
            return jnp.concatenate(per_q, axis=0) + pe_ref[r:r + 1, half * dh:(half + 1) * dh]

        acc_k = jnp.zeros((nblk, CMP_HIDDEN), F32)
        acc_v = jnp.zeros((nblk, CMP_HIDDEN), F32)
        for r in range(0, CMP_BLOCK, 2):
            xk = jnp.concatenate([rows(0, r), rows(0, r + 1)], axis=1)
            xv = jnp.concatenate([rows(1, r), rows(1, r + 1)], axis=1)
            acc_k = acc_k + _bdot(xk, w1k_ref[r * dh:(r + 2) * dh, :])
            acc_v = acc_v + _bdot(xv, w1v_ref[r * dh:(r + 2) * dh, :])
        kc, vc = _compress_tail(acc_k, acc_v, w2k_ref, w2v_ref, gk_ref)
        for q in range(BLOCKS_PER_PAGE):
            kc_ref[pl.ds(q, pp, stride=BLOCKS_PER_PAGE), :] = kc[q * pp:(q + 1) * pp, :]
            vc_ref[pl.ds(q, pp, stride=BLOCKS_PER_PAGE), :] = vc[q * pp:(q + 1) * pp, :]

    full = lambda a: pl.BlockSpec(a.shape, lambda i, c, pt: (0,) * a.ndim)
    gs = pltpu.PrefetchScalarGridSpec(
        num_scalar_prefetch=1, grid=(b, nc),
        in_specs=[pl.BlockSpec(memory_space=pl.ANY), full(pe2), full(w1k), full(w1v), full(w2k), full(w2v), full(g_k)],
        out_specs=[pl.BlockSpec((None, nblk, dh), lambda i, c, pt: (i, c, 0)),
                   pl.BlockSpec((None, nblk, dh), lambda i, c, pt: (i, c, 0))],
        scratch_shapes=[pltpu.VMEM((2, 2, pp * PAGE_PITCH, dh), F32), pltpu.SemaphoreType.DMA((2,))])
    shp = jax.ShapeDtypeStruct((b, n_pages * BLOCKS_PER_PAGE, dh), F32)
    return pl.pallas_call(body, grid_spec=gs, out_shape=[shp, shp],
                          compiler_params=_cparams(("arbitrary", "arbitrary")), name="nsa_compress_paged")(
        page_table, cache, pe2, w1k, w1v, w2k, w2v, g_k)


def _masked_softmax(l, mask):
    l = jnp.where(mask, l, -jnp.inf)
    mx = jnp.max(l, axis=-1, keepdims=True)
    mx = jnp.where(mx == -jnp.inf, 0.0, mx)
    p = jnp.where(mask, jnp.exp(l - mx), 0.0)
    return p / jnp.maximum(jnp.sum(p, axis=-1, keepdims=True), 1e-30)


def _select_blocks(imp, n_lanes, n_sel):
    lane = lax.broadcasted_iota(I32, imp.shape, 1)
    picked = jnp.zeros(imp.shape, F32)
    ids = []
    for _ in range(n_sel):
        mx = jnp.max(imp, axis=-1, keepdims=True)
        idx = jnp.min(jnp.where(imp == mx, lane, n_lanes), axis=-1, keepdims=True)
        hit = lane == idx
        picked = jnp.where(hit, 1.0, picked)
        imp = jnp.where(hit, -2.0, imp)
        ids.append(idx)
    return picked, ids


def _stack_heads(x, nh, dh):
    return jnp.concatenate([x[:, h * dh:(h + 1) * dh] for h in range(nh)], axis=0)


def _nsa_cmp_select(qn, kcmp, vcmp, pair, rel_bias):
    t = qn.shape[0]
    nc, ns = pair.shape
    nh, dh, tq = NSA_HEADS, NSA_HEAD_DIM, Q_TILE
    n_sel = min(N_SEL, ns)
    scale = dh ** -0.5

    def body(rb_ref, q_ref, kc_ref, vc_ref, pair_ref, o_ref, sel_ref):
        t0 = pl.program_id(0) * tq
        q4 = _stack_heads(q_ref[...], nh, dh)
        lc = (_bdot_nt(q4, kc_ref[...]) * scale).reshape(nh, tq, nc)
        tpos = t0 + lax.broadcasted_iota(I32, (tq, nc), 0)
        dist = tpos - (lax.broadcasted_iota(I32, (tq, nc), 1) * CMP_BLOCK + CMP_BLOCK - 1)
        lc = lc + jnp.stack([_t5_bias(dist, rb_ref, h) for h in range(nh)], axis=0)
        pc = _masked_softmax(lc, (dist >= 0)[None])
        oc = _bdot(pc.reshape(nh * tq, nc), vc_ref[...])
        for h in range(nh):
            o_ref[:, h * dh:(h + 1) * dh] = oc[h * tq:(h + 1) * tq, :]
        imp = jnp.dot(jnp.sum(pc, axis=0), pair_ref[...], precision=HI, preferred_element_type=F32)
        blk = lax.broadcasted_iota(I32, (tq, ns), 1)
        tq_pos = t0 + lax.broadcasted_iota(I32, (tq, ns), 0)
        cur = tq_pos // SEL_BLOCK
        forced = (blk == 0) | (blk == cur) | (blk == cur - 1)
        valid = blk * SEL_BLOCK <= tq_pos
        imp = jnp.where(forced, jnp.inf, jnp.where(valid, imp, -1.0))
        sel_ref[...] = _select_blocks(imp, ns, n_sel)[0]

    gs = pltpu.PrefetchScalarGridSpec(
        num_scalar_prefetch=0, grid=(t // tq,),
        in_specs=[pl.BlockSpec(memory_space=pltpu.SMEM),
                  pl.BlockSpec((tq, NSA_WIDTH), lambda i: (i, 0)), pl.BlockSpec((nc, dh), lambda i: (0, 0)),
                  pl.BlockSpec((nc, dh), lambda i: (0, 0)), pl.BlockSpec((nc, ns), lambda i: (0, 0))],
        out_specs=[pl.BlockSpec((tq, NSA_WIDTH), lambda i: (i, 0)), pl.BlockSpec((tq, ns), lambda i: (i, 0))])
    return pl.pallas_call(body, grid_spec=gs,
                          out_shape=[jax.ShapeDtypeStruct((t, NSA_WIDTH), F32), jax.ShapeDtypeStruct((t, ns), F32)],
                          compiler_params=_cparams(("arbitrary",)), name="nsa_cmp_select")(rel_bias, qn, kcmp, vcmp, pair)


def _n_bias_tiles(tq, tk):
    d = 0
    while d * tq - (tk - 1) < T5_THR[-1]:
        d += 1
    return d + 1


def _nsa_flash(qn, gates, o_cmp, selmask, sel_kv, win_kv, rel_bias):
    t = qn.shape[0]
    ns = selmask.shape[1]
    nh, dh, tq, tk = NSA_HEADS, NSA_HEAD_DIM, min(FLASH_Q_TILE, t), K_TILE
    n_bias = _n_bias_tiles(tq, tk)
    scale = dh ** -0.5
    spb = tk // SEL_BLOCK
    qi, kj = _causal_schedule(t // tq, tq, tk)

    def body(qi_ref, kj_ref, rb_ref, q_ref, g_ref, oc_ref, sm_ref, sk_ref, wk_ref, o_ref, bt_ref, qs_ref,
             ms, ls, accs, mw, lw, accw):
        i, j = qi_ref[pl.program_id(0)], kj_ref[pl.program_id(0)]
        t0, k0 = i * tq, j * tk
        rel = lax.broadcasted_iota(I32, (tq, tk), 0) - lax.broadcasted_iota(I32, (tq, tk), 1)

        @pl.when(pl.program_id(0) == 0)
        def _():
            for d in range(n_bias):
                for h in range(nh):
                    bt_ref[d, h] = _t5_bias(rel + d * tq, rb_ref, h)

        @pl.when(j == 0)
        def _():
            qs_ref[...] = (_stack_heads(q_ref[...], nh, dh) * scale).astype(BF16)
            for m_ref, l_ref, a_ref in ((ms, ls, accs), (mw, lw, accw)):
                m_ref[...] = jnp.full(m_ref.shape, -jnp.inf, F32)
                l_ref[...] = jnp.zeros(l_ref.shape, F32)
                a_ref[...] = jnp.zeros(a_ref.shape, F32)

        def update(m_ref, l_ref, a_ref, keys, bias, admit, vals):
            kb, vb = keys.astype(BF16), vals.astype(BF16)
            for h in range(nh):
                rows = pl.ds(h * tq, tq)
                s = _bdot_nt(qs_ref[rows, :], kb) + bias[h] + admit
                m_old = m_ref[rows, :]
                m_new = jnp.maximum(m_old, jnp.max(s, axis=-1, keepdims=True))
                m_safe = jnp.where(m_new == -jnp.inf, 0.0, m_new)
                alpha = jnp.exp(m_old - m_safe)
                pr = jnp.exp(s - m_safe)
                l_ref[rows, :] = l_ref[rows, :] * alpha + jnp.sum(pr, axis=-1, keepdims=True)
                a_ref[rows, :] = a_ref[rows, :] * alpha + _bdot(pr, vb)
                m_ref[rows, :] = m_new

        bias = bt_ref[jnp.minimum((t0 - k0) // tq, n_bias - 1)]
        dist = (t0 - k0) + rel
        expand = (lax.broadcasted_iota(I32, (ns, tk), 0) == j * spb + lax.broadcasted_iota(I32, (ns, tk), 1) // SEL_BLOCK)
        chosen = _bdot(sm_ref[...], expand.astype(F32)) > 0.5
        skv = sk_ref[...]
        update(ms, ls, accs, skv[:, 0:dh], bias, jnp.where(chosen & (dist >= 0), 0.0, -jnp.inf), skv[:, dh:KV_W])

        @pl.when(k0 + tk - 1 >= t0 - NSA_WINDOW)
        def _():
            wkv = wk_ref[...]
            update(mw, lw, accw, wkv[:, 0:dh], bias, jnp.where((dist >= 0) & (dist <= NSA_WINDOW), 0.0, -jnp.inf),
                   wkv[:, dh:KV_W])

        @pl.when(j == (t0 + tq - 1) // tk)
        def _():
            o_s = accs[...] / jnp.maximum(ls[...], 1e-30)
            o_w = accw[...] / jnp.maximum(lw[...], 1e-30)
            g = g_ref[...]
            for h in range(nh):
                rows = slice(h * tq, (h + 1) * tq)
                o_ref[:, h * dh:(h + 1) * dh] = (g[:, 3 * h:3 * h + 1] * oc_ref[:, h * dh:(h + 1) * dh]
                                                 + g[:, 3 * h + 1:3 * h + 2] * o_s[rows]
                                                 + g[:, 3 * h + 2:3 * h + 3] * o_w[rows])

    qrow = lambda w: pl.BlockSpec((tq, w), lambda s, qi, kj: (qi[s], 0))
    krow = pl.BlockSpec((tk, KV_W), lambda s, qi, kj: (kj[s], 0))
    gs = pltpu.PrefetchScalarGridSpec(
        num_scalar_prefetch=2, grid=(qi.shape[0],),
        in_specs=[pl.BlockSpec(memory_space=pltpu.SMEM), qrow(NSA_WIDTH), qrow(LANES), qrow(NSA_WIDTH), qrow(ns), krow, krow],
        out_specs=qrow(NSA_WIDTH),
        scratch_shapes=[pltpu.VMEM((n_bias, nh, tq, tk), F32), pltpu.VMEM((nh * tq, dh), BF16)]
        + [pltpu.VMEM((nh * tq, 1), F32), pltpu.VMEM((nh * tq, 1), F32), pltpu.VMEM((nh * tq, dh), F32)] * 2)
    return pl.pallas_call(body, grid_spec=gs, out_shape=jax.ShapeDtypeStruct((t, NSA_WIDTH), F32),
                          compiler_params=_cparams(("arbitrary",)), name="nsa_flash")(
        qi, kj, rel_bias, qn, gates, o_cmp, selmask, sel_kv, win_kv)


def _head_bias_rows(dist, rb_ref):
    n = dist.shape[1]
    return jnp.concatenate([_t5_bias(dist, rb_ref, h) for h in range(NSA_HEADS)]
                           + [jnp.zeros((8 - NSA_HEADS, n), F32)], axis=0)


def _nsa_cmp_select_decode(q8, kcmp, vcmp, pair, rel_bias, past):
    b, ncp, dh = kcmp.shape
    nsp = pair.shape[1]
    ns = (past + 1 + SEL_BLOCK - 1) // SEL_BLOCK
    ns_l = -(-ns // LANES) * LANES
    n_sel = min(N_SEL, ns)
    scale = dh ** -0.5
    sb = math.gcd(b, 8)
    assert past % PAGE_SIZE == 0 and ncp == past // CMP_BLOCK and nsp == past // SEL_BLOCK and ns_l > nsp

    def body(rb_ref, q_ref, kc_ref, vc_ref, pair_ref, o_ref, id_ref):
        dist = past - (lax.broadcasted_iota(I32, (1, ncp), 1) * CMP_BLOCK + CMP_BLOCK - 1)
        bias = _head_bias_rows(dist, rb_ref)
        mask = jnp.broadcast_to(dist >= 0, (8, ncp))
        head_sums = []
        for s in range(sb):
            pc = _masked_softmax(_bdot_nt(q_ref[s], kc_ref[s]) * scale + bias, mask)
            o_ref[s] = _bdot(pc, vc_ref[s])
            head_sums.append(jnp.sum(pc[0:NSA_HEADS], axis=0, keepdims=True))
        imp = jnp.dot(jnp.concatenate(head_sums, axis=0), pair_ref[...], precision=HI,
                      preferred_element_type=F32)
        imp = jnp.concatenate([imp, jnp.zeros((sb, ns_l - nsp), F32)], axis=1)
        blk = lax.broadcasted_iota(I32, (sb, ns_l), 1)
        cur = past // SEL_BLOCK
        forced = (blk == 0) | (blk == cur) | (blk == cur - 1)
        valid = blk * SEL_BLOCK <= past
        imp = jnp.where(blk < ns, jnp.where(forced, jnp.inf, jnp.where(valid, imp, -1.0)), -2.0)
        ids = _select_blocks(imp, ns_l, n_sel)[1]
        lane = lax.broadcasted_iota(I32, (sb, LANES), 1)
        out = jnp.zeros((sb, LANES), I32)
        for r, idx in enumerate(ids):
            out = jnp.where(lane == r, idx, out)
        id_ref[...] = out

    per = lambda r: pl.BlockSpec((sb, r, dh), lambda i: (i, 0, 0))
    gs = pltpu.PrefetchScalarGridSpec(
        num_scalar_prefetch=0, grid=(b // sb,),
        in_specs=[pl.BlockSpec(memory_space=pltpu.SMEM), per(8), per(ncp), per(ncp),
                  pl.BlockSpec((ncp, nsp), lambda i: (0, 0))],
        out_specs=[per(8), pl.BlockSpec((sb, LANES), lambda i: (i, 0))])
    return pl.pallas_call(body, grid_spec=gs,
                          out_shape=[jax.ShapeDtypeStruct((b, 8, dh), F32), jax.ShapeDtypeStruct((b, LANES), I32)],
                          compiler_params=_cparams(("arbitrary",)), name="nsa_cmp_select_decode")(
        rel_bias, q8, kcmp, vcmp, pair)


def _nsa_sel_win_decode(q8, gates, o_cmp, new_sel, new_win, sel_ids, page_table, cache_sel, win_state, rel_bias, layer, past,
                        earlier):
    b, n_pages = page_table.shape
    dh = NSA_HEAD_DIM
    n_sel = min(N_SEL, (past + 1 + SEL_BLOCK - 1) // SEL_BLOCK)
    n_cached = past // SEL_BLOCK
    halves = PAGE_SIZE // SEL_BLOCK
    wr = win_state.shape[2]
    keep = min(NSA_WINDOW, wr + 1)
    scale = dh ** -0.5
    assert wr == min(NSA_WINDOW, past) and keep == wr

    def body(ids_ref, pt_ref, rb_ref, q_ref, g_ref, oc_ref, ns_ref, nw_ref, ws_ref, cache_ref, *rest):
        early, (o_ref, nwin_ref, buf, sem) = rest[:len(earlier)], rest[len(earlier):]
        i = pl.program_id(0)
        if earlier:
            nwin_ref[0:layer] = early[0][...]

        def block_copy(st, sl, r):
            bid = jnp.minimum(ids_ref[st, r], n_cached - 1)
            return pltpu.make_async_copy(
                cache_ref.at[pt_ref[st, bid // halves], layer, pl.ds((bid % halves) * SEL_BLOCK, SEL_BLOCK)],
                buf.at[sl, r], sem.at[sl])

        slot = _gather_pipeline(i, b, block_copy, n_sel)
        q = q_ref[...]
        zero = jnp.zeros((1, 1), I32)

        def attend(parts, new_row, new_penalty):
            sn = jnp.sum(q * new_row[:, 0:dh], axis=-1, keepdims=True) * scale + _head_bias_rows(zero, rb_ref)
            sn = sn + new_penalty
            masked = [jnp.where(v, s, -jnp.inf) for s, v, _ in parts]
            mx = sn
            for s in masked:
                mx = jnp.maximum(mx, jnp.max(s, axis=-1, keepdims=True))
            mx = jnp.where(mx == -jnp.inf, 0.0, mx)
            pn = jnp.exp(sn - mx)
            den = pn
            num = pn * new_row[:, dh:KV_W]
            for s, (_, _, vals) in zip(masked, parts):
                pr = jnp.exp(s - mx)
                den = den + jnp.sum(pr, axis=-1, keepdims=True)
                num = num + _bdot(pr, vals)
            return num / jnp.maximum(den, 1e-30)

        lane_b = lax.broadcasted_iota(I32, (1, SEL_BLOCK), 1)
        parts = []
        n_cur = 0
        for r in range(n_sel):
            bid = ids_ref[i, r]
            kv = buf[slot, r]
            dist = past - (bid * SEL_BLOCK + lane_b) - jnp.where(bid < n_cached, 0, 2 * (past + SEL_BLOCK))
            s = _bdot_nt(q, kv[:, 0:dh]) * scale + _head_bias_rows(dist, rb_ref)
            parts.append((s, dist >= 0, kv[:, dh:KV_W]))
            n_cur = n_cur + jnp.where(bid == past // SEL_BLOCK, 1, 0)
        o_s = attend(parts, ns_ref[...], jnp.where(n_cur > 0, 0.0, -jnp.inf))

        wkv = ws_ref[...]
        dist_w = wr - lax.broadcasted_iota(I32, (1, wr), 1)
        sw = _bdot_nt(q, wkv[:, 0:dh]) * scale + _head_bias_rows(dist_w, rb_ref)
        o_w = attend([(sw, (dist_w >= 0) & (dist_w <= NSA_WINDOW), wkv[:, dh:KV_W])], nw_ref[...], 0.0)
        nwin_ref[layer, 0:keep - 1, :] = ws_ref[wr + 1 - keep:wr, :]
        nwin_ref[layer, keep - 1:keep, :] = nw_ref[...]

        g = jnp.broadcast_to(g_ref[...], (8, LANES))
        lane = lax.broadcasted_iota(I32, (8, LANES), 1)
        row = lax.broadcasted_iota(I32, (8, LANES), 0)
        gcol = [jnp.sum(jnp.where(lane == 3 * row + k, g, 0.0), axis=-1, keepdims=True) for k in range(3)]
        o_ref[...] = gcol[0] * oc_ref[...] + gcol[1] * o_s + gcol[2] * o_w

    per = lambda r, w: pl.BlockSpec((None, r, w), lambda i, ids, pt: (i, 0, 0))
    gs = pltpu.PrefetchScalarGridSpec(
        num_scalar_prefetch=2, grid=(b,),
        in_specs=[pl.BlockSpec(memory_space=pltpu.SMEM), per(8, dh), per(1, LANES), per(8, dh), per(1, KV_W), per(1, KV_W),
                  pl.BlockSpec((None, None, wr, KV_W), lambda i, ids, pt: (i, layer, 0, 0)),
                  pl.BlockSpec(memory_space=pl.ANY)]
        + [pl.BlockSpec((None, layer, keep, KV_W), lambda i, ids, pt: (i, 0, 0, 0)) for _ in earlier],
        out_specs=[per(8, dh), pl.BlockSpec((None, layer + 1, keep, KV_W), lambda i, ids, pt: (i, 0, 0, 0))],
        scratch_shapes=[pltpu.VMEM((2, n_sel, SEL_BLOCK, KV_W), F32), pltpu.SemaphoreType.DMA((2,))])
    return pl.pallas_call(body, grid_spec=gs,
                          out_shape=[jax.ShapeDtypeStruct((b, 8, dh), F32),
                                     jax.ShapeDtypeStruct((b, layer + 1, keep, KV_W), F32)],
                          compiler_params=_cparams(("arbitrary",)), name="nsa_sel_win_decode")(
        sel_ids, page_table, rel_bias, q8, gates, o_cmp, new_sel, new_win, win_state, cache_sel, *earlier)


def _pair_matrix(nc, ns):
    r = SEL_BLOCK // CMP_BLOCK
    return (jnp.arange(nc)[:, None] // r == jnp.arange(ns)[None, :]).astype(F32)


def _nsa_weights(q_norm, k_norm, pe, w1, w2):
    pe2 = jnp.concatenate([pe[0], pe[1]], axis=1)
    return (q_norm[None], k_norm[0][None], k_norm[1][None], k_norm[2][None], pe2,
            w1[0].astype(BF16), w1[1].astype(BF16), w2[0].astype(BF16), w2[1].astype(BF16))


def _nsa_prompt(p, nw, rel_bias, col_blk=0):
    g_q, g_kc, g_ks, g_kw, pe2, w1k, w1v, w2k, w2v = nw
    t = p.shape[0]
    qn, cmp_rows, sel_rows, win_rows, gates = _nsa_prep(p, g_q, g_ks, g_kw, min(t, 512), col_blk)
    nc, ns = t // CMP_BLOCK, t // SEL_BLOCK
    kcmp, vcmp = _nsa_compress(cmp_rows.reshape(nc, CMP_BLOCK * KV_W), pe2.reshape(1, CMP_BLOCK * KV_W),
                               w1k, w1v, w2k, w2v, g_kc)
    o_cmp, selmask = _nsa_cmp_select(qn, kcmp, vcmp, _pair_matrix(nc, ns), rel_bias)
    o = _nsa_flash(qn, gates, o_cmp, selmask, sel_rows, win_rows, rel_bias)
    return o, cmp_rows, sel_rows, win_rows


def _nsa_decode(p, cache_cmp, cache_sel, win_state, page_table, nw, rel_bias, layer, col_blk=0, earlier_win=()):
    g_q, g_kc, g_ks, g_kw, pe2, w1k, w1v, w2k, w2v = nw
    b, n_pages = page_table.shape
    past = n_pages * PAGE_SIZE
    qn, cmp_rows, sel_rows, win_rows, gates = _nsa_prep(p, g_q, g_ks, g_kw, b, col_blk)
    q8 = jnp.pad(qn.reshape(b, NSA_HEADS, NSA_HEAD_DIM), ((0, 0), (0, 8 - NSA_HEADS), (0, 0)))
    kcmp, vcmp = _nsa_compress_paged(cache_cmp, page_table, layer, pe2, w1k, w1v, w2k, w2v, g_kc)
    o_cmp, ids = _nsa_cmp_select_decode(q8, kcmp, vcmp, _pair_matrix(past // CMP_BLOCK, past // SEL_BLOCK), rel_bias, past)
    o8, new_win = _nsa_sel_win_decode(q8, gates[:, None, :], o_cmp, sel_rows[:, None, :], win_rows[:, None, :],
                                      ids[:, :N_SEL], page_table, cache_sel, win_state, rel_bias, layer, past,
                                      earlier_win)
    return o8[:, :NSA_HEADS].reshape(b, NSA_WIDTH), cmp_rows, sel_rows, new_win


def _moe(x, g, wr, br, w_gate, w_up, w_down, layer, tm_route, tm):
    h, route = _moe_router(x, g, wr, br, tm_route)
    eid = route[:, 0:2].astype(I32)
    wts = route[:, 2:4]
    row_src, row_w, slot_dest, tile_e, n_valid = _moe_dispatch(eid, wts, tm)
    ys = _moe_grouped(h, row_src, row_w[:, None], tile_e, n_valid, w_gate, w_up, w_down, layer, tm)
    return _moe_combine(x, ys, slot_dest, min(x.shape[0], 128))


P_COLS = 9 * MLA_P_COLS
NSA_COL_BLK = GDN_P_COLS // NSA_P_COLS
MLA_COL_BLK = P_COLS // MLA_P_COLS - 1
assert GDN_P_COLS % NSA_P_COLS == 0 and (NSA_COL_BLK + 1) * NSA_P_COLS <= MLA_COL_BLK * MLA_P_COLS


def _fused_w_in(w_in):
    n_gdn = 4 * GDN_WIDTH + 2 * GDN_HEADS
    n_nsa = NSA_WIDTH + 6 * NSA_HEAD_DIM + 3 * NSA_HEADS
    n_mla = MLA_Q_LORA + MLA_KV_LORA + MLA_ROPE
    assert w_in.shape[0] == n_gdn + n_nsa + n_mla
    rows = lambda a, b: w_in[a:b].astype(BF16)
    zeros = lambda n: jnp.zeros((n, w_in.shape[1]), BF16)
    o = MLA_COL_BLK * MLA_P_COLS
    return jnp.concatenate([rows(0, n_gdn), zeros(GDN_P_COLS - n_gdn), rows(n_gdn, n_gdn + n_nsa),
                            zeros(o - GDN_P_COLS - n_nsa), rows(n_gdn + n_nsa, n_gdn + n_nsa + n_mla),
                            zeros(P_COLS - o - n_mla)], axis=0)


def _layer_weights(l, norm1, w_in, gdn_conv, gdn_a_log, gdn_dt_bias, gdn_out_norm, nsa_q_norm, nsa_k_norm, nsa_cmp_pe,
                   nsa_cmp_w1, nsa_cmp_w2, mla_q_a_norm, mla_w_uq, mla_qk_norm, mla_kv_norm, mla_krope_norm, mla_w_uk,
                   mla_w_uv, w_out, norm2, moe_w_grp, moe_b_grp, moe_w_exp, moe_b_exp):
    lane8 = lambda v: jnp.zeros((1, LANES), F32).at[0, GDN_HEADS:2 * GDN_HEADS].set(v)
    wr = jnp.zeros((D_MODEL, LANES), F32).at[:, :N_GROUPS].set(moe_w_grp[l]).at[:, N_GROUPS:N_GROUPS + N_EXPERTS].set(moe_w_exp[l])
    br = jnp.zeros((1, LANES), F32).at[0, :N_GROUPS].set(moe_b_grp[l]).at[0, N_GROUPS:N_GROUPS + N_EXPERTS].set(moe_b_exp[l])
    return dict(
        norm1=norm1[l][None], w_in=_fused_w_in(jnp.transpose(w_in, (2, 0, 1))[:, l, :]),
        gdn=(gdn_conv[l], lane8(gdn_a_log[l]), lane8(gdn_dt_bias[l]), gdn_out_norm[l][None]),
        nsa=_nsa_weights(nsa_q_norm[l], nsa_k_norm[l], nsa_cmp_pe[l], nsa_cmp_w1[l], nsa_cmp_w2[l]),
        mla=_mla_weights(mla_q_a_norm[l], mla_w_uq[l], mla_qk_norm[l], mla_kv_norm[l], mla_krope_norm[l], mla_w_uk[l],
                         mla_w_uv[l]),
        w_out=w_out[l].astype(BF16), norm2=norm2[l][None], wr=wr, br=br)


def _prompt_layer(x, l, lw, rel_bias, moe_w):
    t = x.shape[0]
    p = _norm_matmul(x, lw["norm1"], lw["w_in"], 512, MLA_P_COLS)
    o_gdn, s_fin = _gdn_prompt(p, *lw["gdn"])
    new_conv = p[t - (CONV_W - 1):, 0:3 * GDN_WIDTH]
    o_nsa, cmp_rows, sel_rows, win_rows = _nsa_prompt(p, lw["nsa"], rel_bias, NSA_COL_BLK)
    cos, sin = _rope_tables(jnp.arange(t, dtype=I32))
    q, kv = _mla_prep(p, cos, sin, *lw["mla"][:-1], 512, MLA_COL_BLK)
    o_mla = _headwise_mm(_mla_flash(q, kv, 256, 512), lw["mla"][-1])
    x = _out_proj(x, o_gdn, o_nsa, o_mla, lw["w_out"], 512, 512)
    x = _moe(x, lw["norm2"], lw["wr"], lw["br"], *moe_w, l, 512, 256)
    return x, cmp_rows, sel_rows, kv, win_rows[t - min(NSA_WINDOW, t):], s_fin, new_conv


def _sample_layer(x, l, lw, rel_bias, moe_w, caches, states, page_table, earlier):
    b = x.shape[0]
    cache_cmp, cache_sel, cache_mla = caches
    win_state, gdn_state, conv_state = states
    past = page_table.shape[1] * PAGE_SIZE
    p = _norm_matmul(x, lw["norm1"], lw["w_in"], b, MLA_P_COLS)
    o_gdn, new_conv, new_gdn = _gdn_decode(p[:, None, :], conv_state, gdn_state, *lw["gdn"], l,
                                           (earlier[2], earlier[1]) if earlier else ())
    o_nsa, cmp_rows, sel_rows, new_win = _nsa_decode(p, cache_cmp, cache_sel, win_state, page_table, lw["nsa"], rel_bias,
                                                     l, NSA_COL_BLK, earlier[:1])
    cos, sin = _rope_tables(jnp.full((b,), past, I32))
    q, kv = _mla_prep(p, cos, sin, *lw["mla"][:-1], b, MLA_COL_BLK)
    q8 = jnp.pad(jnp.transpose(q, (1, 0, 2)), ((0, 0), (0, 8 - MLA_HEADS), (0, 0)))
    o_lat = _mla_decode(q8, kv[:, None, :], cache_mla, page_table, l)
    o_mla = _headwise_mm(jnp.transpose(o_lat[:, :MLA_HEADS], (1, 0, 2)), lw["mla"][-1])
    x = _out_proj(x, o_gdn[:, 0], o_nsa, o_mla, lw["w_out"], b, 512)
    x = _moe(x, lw["norm2"], lw["wr"], lw["br"], *moe_w, l, b, 32)
    return x, cmp_rows, sel_rows, kv, (new_win, new_gdn, new_conv)


def kernel(x_prompt, x_sample, cache_nsa_cmp, cache_nsa_sel, cache_mla, state_win_kv, state_gdn, state_conv, page_table, rel_bias, norm1, w_in, gdn_conv, gdn_a_log, gdn_dt_bias, gdn_out_norm, nsa_q_norm, nsa_k_norm, nsa_cmp_pe, nsa_cmp_w1, nsa_cmp_w2, mla_q_a_norm, mla_w_uq, mla_qk_norm, mla_kv_norm, mla_krope_norm, mla_w_uk, mla_w_uv, w_out, norm2, moe_w_grp, moe_b_grp, moe_w_exp, moe_b_exp, moe_w_gate, moe_w_up, moe_w_down):
    depth = norm1.shape[0]
    assert x_prompt.shape[0] == 1 and x_sample.shape[1] == 1
    moe_w = (moe_w_gate, moe_w_up, moe_w_down)
    lws = [_layer_weights(l, norm1, w_in, gdn_conv, gdn_a_log, gdn_dt_bias, gdn_out_norm, nsa_q_norm, nsa_k_norm,
                          nsa_cmp_pe, nsa_cmp_w1, nsa_cmp_w2, mla_q_a_norm, mla_w_uq, mla_qk_norm, mla_kv_norm,
                          mla_krope_norm, mla_w_uk, mla_w_uv, w_out, norm2, moe_w_grp, moe_b_grp, moe_w_exp, moe_b_exp)
           for l in range(depth)]

    xp = x_prompt[0]
    p_out = []
    for l in range(depth):
        xp, *rows = _prompt_layer(xp, l, lws[l], rel_bias, moe_w)
        p_out.append(rows)

    b = x_sample.shape[0]
    xs = x_sample[:, 0]
    conv_state = jnp.transpose(state_conv, (0, 2, 1, 3))
    cache_mla_t = jnp.transpose(cache_mla, (0, 1, 3, 2))
    s_out = []
    stacked = ()
    for l in range(depth):
        xs, *rows, stacked = _sample_layer(xs, l, lws[l], rel_bias, moe_w, (cache_nsa_cmp, cache_nsa_sel, cache_mla_t),
                                           (state_win_kv, state_gdn, conv_state), page_table, stacked)
        s_out.append(rows)
    s_win, s_gdn, s_conv = stacked

    def stack_p(k):
        return jnp.stack([p_out[l][k] for l in range(depth)], axis=0)[None]

    def stack_s(k, width):
        return jnp.stack([s_out[l][k] for l in range(depth)], axis=1).reshape(b, depth, 1, width)

    return (xp[None], xs[:, None],
            stack_p(0), stack_p(1), stack_p(2), stack_p(3), stack_p(4), stack_p(5),
            stack_s(0, KV_W), stack_s(1, KV_W), stack_s(2, MLA_LAT),
            s_win, s_gdn, jnp.transpose(s_conv, (0, 2, 1, 3)))
```

```python
import functools
import math

import jax
import jax.numpy as jnp
import numpy as np
from jax import lax
from jax.experimental import pallas as pl
from jax.experimental.pallas import tpu as pltpu

F32, BF16, I32 = jnp.float32, jnp.bfloat16, jnp.int32
HI = lax.Precision.HIGHEST

D_MODEL = 2048
PAGE_SIZE = 128
GDN_HEADS, GDN_DK, GDN_DV = 8, 128, 128
GDN_WIDTH = GDN_HEADS * GDN_DV
CONV_W = 4
GDN_CHUNK = 64
NSA_HEADS, NSA_HEAD_DIM = 4, 128
NSA_WIDTH = NSA_HEADS * NSA_HEAD_DIM
CMP_BLOCK, CMP_HIDDEN, SEL_BLOCK, N_SEL, NSA_WINDOW = 32, 256, 64, 16, 512
MLA_HEADS, MLA_Q_LORA, MLA_KV_LORA, MLA_NOPE, MLA_ROPE, MLA_V = 4, 512, 128, 128, 32, 128
MLA_QK = MLA_NOPE + MLA_ROPE
MLA_LAT = MLA_KV_LORA + MLA_ROPE
ROPE_THETA = 10000.0
N_BUCKETS, MAX_DISTANCE = 32, 128
N_GROUPS, EXPERTS_PER_GROUP = 8, 8
N_EXPERTS = N_GROUPS * EXPERTS_PER_GROUP
D_EXPERT = 256
RMS_EPS = 1e-6
L2_EPS = 1e-6
Q_TILE = 256
FLASH_Q_TILE = 128
K_TILE = 512
LANES = 128
VMEM_LIMIT = 56 << 20


def _cparams(sem, vmem=VMEM_LIMIT):
    return pltpu.CompilerParams(dimension_semantics=sem, vmem_limit_bytes=vmem)


def _bdot(a, b):
    return jnp.dot(a.astype(BF16), b.astype(BF16), preferred_element_type=F32)


def _bdot_nt(a, b):
    return lax.dot_general(a.astype(BF16), b.astype(BF16), (((1,), (1,)), ((), ())), preferred_element_type=F32)


def _einsum_bf16x3(eq, a, b):
    ah, bh = a.astype(BF16), b.astype(BF16)
    al, bl = (a - ah.astype(F32)).astype(BF16), (b - bh.astype(F32)).astype(BF16)
    dot = lambda x, y: jnp.einsum(eq, x, y, preferred_element_type=F32)
    return dot(ah, bh) + (dot(ah, bl) + dot(al, bh))


def _rms(x, g, n=None):
    n = x.shape[-1] if n is None else n
    return x * lax.rsqrt(jnp.sum(x * x, axis=-1, keepdims=True) / n + RMS_EPS) * g


def _sigmoid(x):
    return 1.0 / (1.0 + jnp.exp(-x))


def _t5_thresholds():
    n = np.arange(0, 4 * MAX_DISTANCE)
    exact = N_BUCKETS // 2
    out = []
    for dt in (np.float32, np.float64):
        nf = np.maximum(n, exact).astype(dt)
        large = exact + (np.log(nf / exact) / math.log(MAX_DISTANCE / exact) * (N_BUCKETS - exact)).astype(np.int32)
        b = np.where(n < exact, n, np.minimum(large, N_BUCKETS - 1))
        out.append([int(np.argmax(b >= k)) for k in range(1, N_BUCKETS)])
    assert out[0] == out[1]
    return out[0]


T5_THR = _t5_thresholds()


def _t5_bias(dist, tbl_ref, h):
    acc = jnp.full(dist.shape, tbl_ref[0, h], F32)
    for b in range(1, N_BUCKETS):
        acc = acc + jnp.where(dist >= T5_THR[b - 1], tbl_ref[b, h] - tbl_ref[b - 1, h], 0.0)
    return acc


def _causal_schedule(n_q, tq, tk):
    pairs = [(i, j) for i in range(n_q) for j in range((i * tq + tq - 1) // tk + 1)]
    return jnp.asarray([p[0] for p in pairs], I32), jnp.asarray([p[1] for p in pairs], I32)


def _gather_pipeline(step, n_steps, copies_of, n_copies, rolled=False):
    slot = step % 2

    def each(st, sl, op):
        if rolled:
            def one(k, carry):
                op(copies_of(st, sl, k))
                return carry
            lax.fori_loop(0, n_copies, one, 0, unroll=8)
        else:
            for k in range(n_copies):
                op(copies_of(st, sl, k))

    @pl.when(step == 0)
    def _():
        each(0, 0, lambda cp: cp.start())

    @pl.when(step + 1 < n_steps)
    def _():
        each(step + 1, 1 - slot, lambda cp: cp.start())

    @pl.when(step < n_steps)
    def _():
        each(step, slot, lambda cp: cp.wait())

    return slot


def _norm_matmul(x, g, w, tm, tn):
    m, d = x.shape
    n = w.shape[0]

    def body(x_ref, g_ref, w_ref, o_ref, h_ref):
        @pl.when(pl.program_id(1) == 0)
        def _():
            h_ref[...] = _rms(x_ref[...], g_ref[...]).astype(BF16)

        o_ref[...] = _bdot_nt(h_ref[...], w_ref[...])

    return pl.pallas_call(
        body, grid=(m // tm, n // tn),
        in_specs=[pl.BlockSpec((tm, d), lambda i, j: (i, 0)), pl.BlockSpec((1, d), lambda i, j: (0, 0)),
                  pl.BlockSpec((tn, d), lambda i, j: (j, 0))],
        out_specs=pl.BlockSpec((tm, tn), lambda i, j: (i, j)),
        out_shape=jax.ShapeDtypeStruct((m, n), F32),
        scratch_shapes=[pltpu.VMEM((tm, d), BF16)],
        compiler_params=_cparams(("arbitrary", "arbitrary")), name="norm_matmul")(x, g, w)


def _out_proj(x, a1, a2, a3, w, tm, tn):
    m, d = x.shape
    k1, k2, k3 = a1.shape[1], a2.shape[1], a3.shape[1]
    assert k1 % k2 == 0 and k2 == k3

    def body(x_ref, a1_ref, a2_ref, a3_ref, w1_ref, w2_ref, w3_ref, o_ref):
        o_ref[...] = (x_ref[...] + _bdot(a1_ref[...], w1_ref[...]) + _bdot(a2_ref[...], w2_ref[...])
                      + _bdot(a3_ref[...], w3_ref[...]))

    return pl.pallas_call(
        body, grid=(m // tm, d // tn),
        in_specs=[pl.BlockSpec((tm, tn), lambda i, j: (i, j)),
                  pl.BlockSpec((tm, k1), lambda i, j: (i, 0)), pl.BlockSpec((tm, k2), lambda i, j: (i, 0)),
                  pl.BlockSpec((tm, k3), lambda i, j: (i, 0)),
                  pl.BlockSpec((k1, tn), lambda i, j: (0, j)),
                  pl.BlockSpec((k2, tn), lambda i, j: (k1 // k2, j)),
                  pl.BlockSpec((k3, tn), lambda i, j: (k1 // k2 + 1, j))],
        out_specs=pl.BlockSpec((tm, tn), lambda i, j: (i, j)),
        out_shape=jax.ShapeDtypeStruct((m, d), F32),
        compiler_params=_cparams(("arbitrary", "arbitrary")), name="out_proj")(x, a1, a2, a3, w, w, w)


def _headwise_mm(x, w):
    h, m, k = x.shape
    n = w.shape[2]
    tm = min(m, 512)

    def body(x_ref, w_ref, o_ref):
        for i in range(h):
            o_ref[:, i * n:(i + 1) * n] = _bdot(x_ref[i], w_ref[i])

    return pl.pallas_call(
        body, grid=(m // tm,),
        in_specs=[pl.BlockSpec((h, tm, k), lambda i: (0, i, 0)), pl.BlockSpec((h, k, n), lambda i: (0, 0, 0))],
        out_specs=pl.BlockSpec((tm, h * n), lambda i: (i, 0)),
        out_shape=jax.ShapeDtypeStruct((m, h * n), F32),
        compiler_params=_cparams(("arbitrary",)), name="headwise_mm")(x, w)


def _moe_router(x, g, wr, br, tm):
    m, d = x.shape

    def body(x_ref, g_ref, wr_ref, br_ref, h_ref, r_ref):
        h = _rms(x_ref[...], g_ref[...])
        h_ref[...] = h
        lg = jnp.dot(h, wr_ref[...], precision=HI, preferred_element_type=F32) + br_ref[...]
        lane = lax.broadcasted_iota(I32, lg.shape, 1)
        is_g = lane < N_GROUPS
        lgm = jnp.where(is_g, lg, -jnp.inf)
        mg = jnp.max(lgm, axis=-1, keepdims=True)
        p_top = 1.0 / jnp.sum(jnp.where(is_g, jnp.exp(lgm - mg), 0.0), axis=-1, keepdims=True)
        gidx = jnp.min(jnp.where(lgm == mg, lane, LANES), axis=-1, keepdims=True)
        in_grp = (lane >= N_GROUPS) & (lane < N_GROUPS + N_EXPERTS) & (((lane - N_GROUPS) // EXPERTS_PER_GROUP) == gidx)
        le = jnp.where(in_grp, lg, -jnp.inf)
        m1 = jnp.max(le, axis=-1, keepdims=True)
        i1 = jnp.min(jnp.where(le == m1, lane, LANES), axis=-1, keepdims=True)
        le2 = jnp.where(lane == i1, -jnp.inf, le)
        m2 = jnp.max(le2, axis=-1, keepdims=True)
        i2 = jnp.min(jnp.where(le2 == m2, lane, LANES), axis=-1, keepdims=True)
        e2 = jnp.exp(m2 - m1)
        w1 = p_top / (1.0 + e2)
        w2 = p_top * e2 / (1.0 + e2)
        r_ref[...] = jnp.where(lane == 0, (i1 - N_GROUPS).astype(F32),
                               jnp.where(lane == 1, (i2 - N_GROUPS).astype(F32),
                                         jnp.where(lane == 2, w1, jnp.where(lane == 3, w2, 0.0))))

    return pl.pallas_call(
        body, grid=(m // tm,),
        in_specs=[pl.BlockSpec((tm, d), lambda i: (i, 0)), pl.BlockSpec((1, d), lambda i: (0, 0)),
                  pl.BlockSpec((d, LANES), lambda i: (0, 0)), pl.BlockSpec((1, LANES), lambda i: (0, 0))],
        out_specs=[pl.BlockSpec((tm, d), lambda i: (i, 0)), pl.BlockSpec((tm, LANES), lambda i: (i, 0))],
        out_shape=[jax.ShapeDtypeStruct((m, d), F32), jax.ShapeDtypeStruct((m, LANES), F32)],
        compiler_params=_cparams(("arbitrary",)), name="moe_router")(x, g, wr, br)


def _moe_dispatch(eid, wts, tm):
    t = eid.shape[0]
    n = 2 * t
    n_tiles = n // tm + N_EXPERTS
    e_flat = eid.reshape(n)
    order = jnp.argsort(e_flat, stable=True)
    e_s = e_flat[order]
    counts = jnp.sum(jax.nn.one_hot(e_flat, N_EXPERTS, dtype=I32), axis=0)
    tiles_e = (counts + tm - 1) // tm
    tile_end = jnp.cumsum(tiles_e)
    grp_start = jnp.cumsum(counts) - counts
    dest_sorted = (tile_end - tiles_e)[e_s] * tm + (jnp.arange(n, dtype=I32) - grp_start[e_s])
    row_src = jnp.zeros((n_tiles * tm,), I32).at[dest_sorted].set((order // 2).astype(I32))
    row_w = jnp.zeros((n_tiles * tm,), F32).at[dest_sorted].set(wts.reshape(n)[order])
    slot_dest = jnp.zeros((n,), I32).at[order].set(dest_sorted).reshape(t, 2)
    n_valid = tile_end[-1]
    ti = jnp.arange(n_tiles, dtype=I32)
    tile_e = jnp.minimum(jnp.searchsorted(tile_end, ti, side="right").astype(I32), N_EXPERTS - 1)
    tile_e = jnp.where(ti < n_valid, tile_e, tile_e[n_valid - 1])
    return row_src, row_w, slot_dest, tile_e, n_valid.reshape(1).astype(I32)


def _moe_grouped(h, row_src, ws, tile_e, n_valid, w_gate, w_up, w_down, layer, tm):
    d = h.shape[1]
    r = row_src.shape[0]
    f = w_gate.shape[-1]

    def body(te_ref, nv_ref, rs_ref, h_ref, w_ref, wg_ref, wu_ref, wd_ref, o_ref, xbuf, sem):
        def row_copy(st, sl, k):
            return pltpu.make_async_copy(h_ref.at[pl.ds(rs_ref[st * tm + k], 1)], xbuf.at[sl, pl.ds(k, 1)], sem.at[sl])

        slot = _gather_pipeline(pl.program_id(0), nv_ref[0], row_copy, tm, rolled=True)

        @pl.when(pl.program_id(0) < nv_ref[0])
        def _():
            x = xbuf[slot]
            a = _bdot(x, wg_ref[...])
            b = _bdot(x, wu_ref[...])
            act = a * _sigmoid(a) * b * w_ref[...]
            o_ref[...] = _bdot(act, wd_ref[...])

        @pl.when(pl.program_id(0) >= nv_ref[0])
        def _():
            o_ref[...] = jnp.zeros(o_ref.shape, F32)

    def wmap(i, te, nv, rs):
        return (layer, te[i], 0, 0)

    gs = pltpu.PrefetchScalarGridSpec(
        num_scalar_prefetch=3, grid=(r // tm,),
        in_specs=[pl.BlockSpec(memory_space=pl.ANY),
                  pl.BlockSpec((tm, 1), lambda i, te, nv, rs: (jnp.minimum(i, nv[0] - 1), 0)),
                  pl.BlockSpec((None, None, d, f), wmap), pl.BlockSpec((None, None, d, f), wmap),
                  pl.BlockSpec((None, None, f, d), wmap)],
        out_specs=pl.BlockSpec((tm, d), lambda i, te, nv, rs: (i, 0)),
        scratch_shapes=[pltpu.VMEM((2, tm, d), F32), pltpu.SemaphoreType.DMA((2,))])
    return pl.pallas_call(body, grid_spec=gs, out_shape=jax.ShapeDtypeStruct((r, d), F32),
                          compiler_params=_cparams(("arbitrary",)), name="moe_grouped")(
        tile_e, n_valid, row_src, h, ws, w_gate, w_up, w_down)


def _moe_combine(x, ys, slot_dest, tc):
    t, d = x.shape
    n_steps = t // tc
    dest = slot_dest.reshape(n_steps, tc, 2).transpose(0, 2, 1).reshape(-1)

    def body(d_ref, x_ref, ys_ref, o_ref, buf, sem):
        def row_copy(st, sl, k):
            return pltpu.make_async_copy(ys_ref.at[pl.ds(d_ref[st * 2 * tc + k], 1)], buf.at[sl, pl.ds(k, 1)], sem.at[sl])

        slot = _gather_pipeline(pl.program_id(0), n_steps, row_copy, 2 * tc, rolled=True)
        o_ref[...] = x_ref[...] + buf[slot, 0:tc, :] + buf[slot, tc:2 * tc, :]

    gs = pltpu.PrefetchScalarGridSpec(
        num_scalar_prefetch=1, grid=(n_steps,),
        in_specs=[pl.BlockSpec((tc, d), lambda i, dr: (i, 0)), pl.BlockSpec(memory_space=pl.ANY)],
        out_specs=pl.BlockSpec((tc, d), lambda i, dr: (i, 0)),
        scratch_shapes=[pltpu.VMEM((2, 2 * tc, d), F32), pltpu.SemaphoreType.DMA((2,))])
    return pl.pallas_call(body, grid_spec=gs, out_shape=jax.ShapeDtypeStruct((t, d), F32),
                          compiler_params=_cparams(("arbitrary",)), name="moe_combine")(dest, x, ys)


GDN_P_COLS = 4 * GDN_WIDTH + LANES


def _softplus(x):
    return jnp.maximum(x, 0.0) + jnp.log(1.0 + jnp.exp(-jnp.abs(x)))


def _gdn_gates(ba, al, dtb):
    return _sigmoid(ba), -jnp.exp(al) * _softplus(ba + dtb)


def _gdn_prompt(p, conv_w, a_log, dt_bias, out_norm):
    t = p.shape[0]
    c, nh, dk = GDN_CHUNK, GDN_HEADS, GDN_DK
    w3 = 3 * GDN_WIDTH

    def body(qkv_ref, z_ref, ba_ref, cw_ref, al_ref, dtb_ref, gn_ref, o_ref, sfin_ref, xbuf, s_ref):
        i = pl.program_id(0)

        @pl.when(i == 0)
        def _():
            xbuf[0:8, :] = jnp.zeros((8, w3), F32)
            s_ref[...] = jnp.zeros(s_ref.shape, F32)

        xbuf[8:8 + c, :] = qkv_ref[...]
        acc = cw_ref[0:1, :] * xbuf[5:5 + c, :]
        for j in range(1, CONV_W):
            acc = acc + cw_ref[j:j + 1, :] * xbuf[5 + j:5 + j + c, :]
        xbuf[0:8, :] = xbuf[c:c + 8, :]
        qkv = acc * _sigmoid(acc)

        def heads(off):
            return jnp.stack([qkv[:, off + h * dk: off + (h + 1) * dk] for h in range(nh)], axis=0)

        q, k, v = heads(0), heads(GDN_WIDTH), heads(2 * GDN_WIDTH)
        q = q * lax.rsqrt(jnp.sum(q * q, axis=-1, keepdims=True) + L2_EPS) * (dk ** -0.5)
        k = k * lax.rsqrt(jnp.sum(k * k, axis=-1, keepdims=True) + L2_EPS)
        beta_l, g_l = _gdn_gates(ba_ref[...], al_ref[...], dtb_ref[...])
        row = lax.broadcasted_iota(I32, (c, c), 0)
        col = lax.broadcasted_iota(I32, (c, c), 1)
        tril = (row >= col).astype(F32)
        gc_l = jnp.dot(tril, g_l, precision=HI, preferred_element_type=F32)
        beta = jnp.stack([beta_l[:, h:h + 1] for h in range(nh)], axis=0)
        gcol = jnp.stack([gc_l[:, nh + h:nh + h + 1] for h in range(nh)], axis=0)
        eye = (row == col)[None]
        grow = jnp.sum(jnp.where(eye, gcol, 0.0), axis=1, keepdims=True)
        glast = gcol[:, c - 1:c, :]
        causal = (row >= col)[None]
        strict = (row > col)[None]
        decay = jnp.exp(jnp.where(causal, gcol - grow, -jnp.inf))
        eg = jnp.exp(gcol)
        kb = k * beta
        kk = jnp.einsum("hid,hjd->hij", kb.astype(BF16), k.astype(BF16), preferred_element_type=F32)
        nmat = -jnp.where(strict, kk * decay, 0.0)
        tmat = jnp.where(eye, 1.0, 0.0) + nmat
        pw = nmat
        for _ in range(5):
            pw = _einsum_bf16x3("hij,hjk->hik", pw, pw)
            tmat = tmat + _einsum_bf16x3("hij,hjk->hik", tmat, pw)
        tb = tmat.astype(BF16)
        u = jnp.einsum("hij,hjd->hid", tb, (v * beta).astype(BF16), preferred_element_type=F32)
        w = jnp.einsum("hij,hjd->hid", tb, (kb * eg).astype(BF16), preferred_element_type=F32)
        attn = jnp.einsum("hid,hjd->hij", q.astype(BF16), k.astype(BF16), preferred_element_type=F32) * decay
        qg = q * eg
        kd = k * jnp.exp(glast - gcol)
        s = s_ref[...]
        sb = s.astype(BF16)
        v_new = u - jnp.einsum("hik,hkd->hid", w.astype(BF16), sb, preferred_element_type=F32)
        vb = v_new.astype(BF16)
        o = (jnp.einsum("hik,hkd->hid", qg.astype(BF16), sb, preferred_element_type=F32)
             + jnp.einsum("hij,hjd->hid", attn.astype(BF16), vb, preferred_element_type=F32))
        s_new = s * jnp.exp(glast) + jnp.einsum("hik,hid->hkd", kd.astype(BF16), vb, preferred_element_type=F32)
        s_ref[...] = s_new
        sfin_ref[...] = s_new
        z = z_ref[...]
        for h in range(nh):
            zh = z[:, h * dk:(h + 1) * dk]
            o_ref[:, h * dk:(h + 1) * dk] = _rms(o[h], gn_ref[...]) * (zh * _sigmoid(zh))

    return pl.pallas_call(
        body, grid=(t // c,),
        in_specs=[pl.BlockSpec((c, w3), lambda i: (i, 0)), pl.BlockSpec((c, GDN_WIDTH), lambda i: (i, 3)),
                  pl.BlockSpec((c, LANES), lambda i: (i, 4 * GDN_WIDTH // LANES)),
                  pl.BlockSpec((CONV_W, w3), lambda i: (0, 0)),
                  pl.BlockSpec((1, LANES), lambda i: (0, 0)), pl.BlockSpec((1, LANES), lambda i: (0, 0)),
                  pl.BlockSpec((1, GDN_DV), lambda i: (0, 0))],
        out_specs=[pl.BlockSpec((c, GDN_WIDTH), lambda i: (i, 0)),
                   pl.BlockSpec((nh, dk, GDN_DV), lambda i: (0, 0, 0))],
        out_shape=[jax.ShapeDtypeStruct((t, GDN_WIDTH), F32), jax.ShapeDtypeStruct((nh, dk, GDN_DV), F32)],
        scratch_shapes=[pltpu.VMEM((c + 8, w3), F32), pltpu.VMEM((nh, dk, GDN_DV), F32)],
        compiler_params=_cparams(("arbitrary",)), name="gdn_prompt")(p, p, p, conv_w, a_log, dt_bias, out_norm)


def _gdn_decode(p3, conv_state, state, conv_w, a_log, dt_bias, out_norm, layer, earlier):
    b = p3.shape[0]
    nh, dk = GDN_HEADS, GDN_DK
    w3 = 3 * GDN_WIDTH
    ne = layer if earlier else 0
    assert ne == layer

    def body(qkv_ref, z_ref, ba_ref, cs_ref, st_ref, cw_ref, al_ref, dtb_ref, gn_ref, *rest):
        early, (o_ref, ncs_ref, nst_ref) = rest[:len(earlier)], rest[len(earlier):]
        if earlier:
            ncs_ref[:, 0:ne, :] = early[0][...]
            nst_ref[0:ne] = early[1][...]
        x = qkv_ref[...]
        rows = [cs_ref[j, layer:layer + 1, :] for j in range(CONV_W - 1)] + [x]
        acc = cw_ref[0:1, :] * rows[0]
        for j in range(1, CONV_W):
            acc = acc + cw_ref[j:j + 1, :] * rows[j]
        for j in range(CONV_W - 1):
            ncs_ref[j, ne:ne + 1, :] = rows[j + 1]
        qkv = acc * _sigmoid(acc)

        def heads(off):
            return jnp.concatenate([qkv[:, off + h * dk: off + (h + 1) * dk] for h in range(nh)], axis=0)

        q, k, v = heads(0), heads(GDN_WIDTH), heads(2 * GDN_WIDTH)
        q = q * lax.rsqrt(jnp.sum(q * q, axis=-1, keepdims=True) + L2_EPS) * (dk ** -0.5)
        k = k * lax.rsqrt(jnp.sum(k * k, axis=-1, keepdims=True) + L2_EPS)
        qk = jnp.sum(q * k, axis=-1, keepdims=True)
        beta_l, g_l = _gdn_gates(ba_ref[...], al_ref[...], dtb_ref[...])
        kt = k.T
        qt = q.T
        z = z_ref[...]
        for h in range(nh):
            s = st_ref[h]
            bh = beta_l[:, h:h + 1]
            egh = jnp.exp(g_l[:, nh + h:nh + h + 1])
            kc = kt[:, h:h + 1]
            qc = qt[:, h:h + 1]
            ks = jnp.sum(kc * s, axis=0, keepdims=True)
            qs = jnp.sum(qc * s, axis=0, keepdims=True)
            v_new = v[h:h + 1, :] * bh - ks * (bh * egh)
            o = qs * egh + qk[h:h + 1, :] * v_new
            nst_ref[ne, h] = s * egh + kc * v_new
            zh = z[:, h * dk:(h + 1) * dk]
            o_ref[:, h * dk:(h + 1) * dk] = _rms(o, gn_ref[...]) * (zh * _sigmoid(zh))

    per_sample = lambda shape: pl.BlockSpec((None,) + shape, lambda i: (i,) + (0,) * len(shape))
    conv_out, state_out = (CONV_W - 1, ne + 1, w3), (ne + 1, nh, dk, GDN_DV)
    return pl.pallas_call(
        body, grid=(b,),
        in_specs=[pl.BlockSpec((None, 1, w3), lambda i: (i, 0, 0)),
                  pl.BlockSpec((None, 1, GDN_WIDTH), lambda i: (i, 0, 3)),
                  pl.BlockSpec((None, 1, LANES), lambda i: (i, 0, 4 * GDN_WIDTH // LANES)),
                  per_sample(conv_state.shape[1:]),
                  pl.BlockSpec((None, None, nh, dk, GDN_DV), lambda i: (i, layer, 0, 0, 0)),
                  pl.BlockSpec((CONV_W, w3), lambda i: (0, 0)),
                  pl.BlockSpec((1, LANES), lambda i: (0, 0)), pl.BlockSpec((1, LANES), lambda i: (0, 0)),
                  pl.BlockSpec((1, GDN_DV), lambda i: (0, 0))] + [per_sample(a.shape[1:]) for a in earlier],
        out_specs=[pl.BlockSpec((None, 1, GDN_WIDTH), lambda i: (i, 0, 0)), per_sample(conv_out), per_sample(state_out)],
        out_shape=[jax.ShapeDtypeStruct((b, 1, GDN_WIDTH), F32), jax.ShapeDtypeStruct((b,) + conv_out, F32),
                   jax.ShapeDtypeStruct((b,) + state_out, F32)],
        compiler_params=_cparams(("arbitrary",)), name="gdn_decode")(
        p3, p3, p3, conv_state, state, conv_w, a_log, dt_bias, out_norm, *earlier)


MLA_P_COLS = MLA_Q_LORA + 2 * LANES


def _rope_tables(pos):
    half = MLA_ROPE // 2
    inv = ROPE_THETA ** (-jnp.arange(half, dtype=F32) / half)
    ang = jnp.tile(pos.astype(F32)[:, None] * inv, (1, LANES // half))
    sign = jnp.where((jnp.arange(LANES) % MLA_ROPE) < half, -1.0, 1.0).astype(F32)
    return jnp.cos(ang), jnp.sin(ang) * sign


def _mla_weights(q_a_norm, w_uq, qk_norm, kv_norm, krope_norm, w_uk, w_uv):
    nh = MLA_HEADS
    w4 = w_uq.reshape(MLA_Q_LORA, nh, MLA_QK)
    w_uq_p = jnp.concatenate([w4[:, :, :MLA_NOPE].reshape(MLA_Q_LORA, nh * MLA_NOPE),
                              w4[:, :, MLA_NOPE:].reshape(MLA_Q_LORA, nh * MLA_ROPE)], axis=1).astype(BF16)
    g_nope = qk_norm[None, :MLA_NOPE]
    g_rope = jnp.tile(qk_norm[MLA_NOPE:], nh)[None]
    w_uk_t = jnp.transpose(w_uk, (1, 2, 0)).astype(BF16)
    w_uv_t = jnp.transpose(w_uv, (1, 0, 2)).astype(BF16)
    g_kr = jnp.pad(krope_norm, (0, LANES - MLA_ROPE))[None]
    return q_a_norm[None], w_uq_p, g_nope, g_rope, w_uk_t, kv_norm[None], g_kr, w_uv_t


def _rope_rotate(x, cos, sin_signed):
    lane = lax.broadcasted_iota(I32, x.shape, 1)
    half = MLA_ROPE // 2
    swapped = jnp.where(lane % MLA_ROPE < half, pltpu.roll(x, LANES - half, 1), pltpu.roll(x, half, 1))
    return x * cos + swapped * sin_signed


def _mla_prep(p, cos, sin, g_qa, w_uq, g_nope, g_rope, w_uk, g_kv, g_kr, tm, col_blk=0):
    m = p.shape[0]
    nh = MLA_HEADS

    def body(p_ref, cos_ref, sin_ref, gqa_ref, wuq_ref, gn_ref, gr_ref, wuk_ref, gkv_ref, gkr_ref, q_ref, kv_ref):
        cos, sin = cos_ref[...], sin_ref[...]
        cq = _rms(p_ref[:, 0:MLA_Q_LORA], gqa_ref[...])
        qf = _bdot(cq, wuq_ref[...])
        rope_all = qf[:, nh * MLA_NOPE:]
        r2 = rope_all * rope_all
        lane = lax.broadcasted_iota(I32, rope_all.shape, 1)
        inv_lane = jnp.zeros(rope_all.shape, F32)
        for h in range(nh):
            nope = qf[:, h * MLA_NOPE:(h + 1) * MLA_NOPE]
            in_h = lane // MLA_ROPE == h
            ss = jnp.sum(nope * nope, axis=-1, keepdims=True) + jnp.sum(jnp.where(in_h, r2, 0.0), axis=-1, keepdims=True)
            inv = lax.rsqrt(ss / MLA_QK + RMS_EPS)
            inv_lane = jnp.where(in_h, inv, inv_lane)
            q_ref[h, :, 0:MLA_KV_LORA] = _bdot(nope * inv * gn_ref[...], wuk_ref[h])
        rot = _rope_rotate(rope_all * inv_lane * gr_ref[...], cos, sin)
        for h in range(nh):
            q_ref[h, :, MLA_KV_LORA:MLA_LAT] = rot[:, h * MLA_ROPE:(h + 1) * MLA_ROPE]
        kv_ref[:, 0:MLA_KV_LORA] = _rms(p_ref[:, MLA_Q_LORA:MLA_Q_LORA + MLA_KV_LORA], gkv_ref[...])
        kr = p_ref[:, MLA_Q_LORA + MLA_KV_LORA:MLA_P_COLS]
        kv_ref[:, MLA_KV_LORA:MLA_LAT] = _rope_rotate(_rms(kr, gkr_ref[...], MLA_ROPE), cos, sin)[:, 0:MLA_ROPE]

    full = lambda a: pl.BlockSpec(a.shape, lambda i: (0,) * a.ndim)
    return pl.pallas_call(
        body, grid=(m // tm,),
        in_specs=[pl.BlockSpec((tm, MLA_P_COLS), lambda i: (i, col_blk)), pl.BlockSpec((tm, LANES), lambda i: (i, 0)),
                  pl.BlockSpec((tm, LANES), lambda i: (i, 0)), full(g_qa), full(w_uq), full(g_nope), full(g_rope),
                  full(w_uk), full(g_kv), full(g_kr)],
        out_specs=[pl.BlockSpec((nh, tm, MLA_LAT), lambda i: (0, i, 0)), pl.BlockSpec((tm, MLA_LAT), lambda i: (i, 0))],
        out_shape=[jax.ShapeDtypeStruct((nh, m, MLA_LAT), F32), jax.ShapeDtypeStruct((m, MLA_LAT), F32)],
        compiler_params=_cparams(("arbitrary",)), name="mla_prep")(
        p, cos, sin, g_qa, w_uq, g_nope, g_rope, w_uk, g_kv, g_kr)


def _mla_flash(q, kv, tq, tk):
    nh, t, _ = q.shape
    scale = MLA_QK ** -0.5
    qi, kj = _causal_schedule(t // tq, tq, tk)

    def body(qi_ref, kj_ref, q_ref, kv_ref, o_ref, qs_ref, m_ref, l_ref, acc_ref):
        i, j = qi_ref[pl.program_id(0)], kj_ref[pl.program_id(0)]

        @pl.when(j == 0)
        def _():
            qs_ref[...] = (q_ref[...].reshape(nh * tq, MLA_LAT) * scale).astype(BF16)
            m_ref[...] = jnp.full(m_ref.shape, -jnp.inf, F32)
            l_ref[...] = jnp.zeros(l_ref.shape, F32)
            acc_ref[...] = jnp.zeros(acc_ref.shape, F32)

        kb = kv_ref[...].astype(BF16)
        rel = (i * tq - j * tk) + lax.broadcasted_iota(I32, (tq, tk), 0) - lax.broadcasted_iota(I32, (tq, tk), 1)
        causal = jnp.where(rel >= 0, 0.0, -jnp.inf)
        s = (_bdot_nt(qs_ref[...], kb).reshape(nh, tq, tk) + causal[None]).reshape(nh * tq, tk)
        m_new = jnp.maximum(m_ref[...], jnp.max(s, axis=-1, keepdims=True))
        alpha = jnp.exp(m_ref[...] - m_new)
        pr = jnp.exp(s - m_new)
        l_ref[...] = l_ref[...] * alpha + jnp.sum(pr, axis=-1, keepdims=True)
        acc_ref[...] = acc_ref[...] * alpha + jnp.dot(pr.astype(BF16), kb[:, 0:MLA_KV_LORA], preferred_element_type=F32)
        m_ref[...] = m_new

        @pl.when(j == (i * tq + tq - 1) // tk)
        def _():
            o_ref[...] = (acc_ref[...] / l_ref[...]).reshape(nh, tq, MLA_KV_LORA)

    gs = pltpu.PrefetchScalarGridSpec(
        num_scalar_prefetch=2, grid=(qi.shape[0],),
        in_specs=[pl.BlockSpec((nh, tq, MLA_LAT), lambda s, qi, kj: (0, qi[s], 0)),
                  pl.BlockSpec((tk, MLA_LAT), lambda s, qi, kj: (kj[s], 0))],
        out_specs=pl.BlockSpec((nh, tq, MLA_KV_LORA), lambda s, qi, kj: (0, qi[s], 0)),
        scratch_shapes=[pltpu.VMEM((nh * tq, MLA_LAT), BF16), pltpu.VMEM((nh * tq, 1), F32),
                        pltpu.VMEM((nh * tq, 1), F32), pltpu.VMEM((nh * tq, MLA_KV_LORA), F32)])
    return pl.pallas_call(body, grid_spec=gs, out_shape=jax.ShapeDtypeStruct((nh, t, MLA_KV_LORA), F32),
                          compiler_params=_cparams(("arbitrary",)), name="mla_flash")(qi, kj, q, kv)


PAGES_PER_STEP = 128


def _mla_decode(q8, knew, cache, page_table, layer):
    b, n_pages = page_table.shape
    pp = min(PAGES_PER_STEP, n_pages)
    nc = n_pages // pp
    scale = MLA_QK ** -0.5

    def body(pt_ref, q_ref, kn_ref, cache_ref, o_ref, buf, sem, m_ref, l_ref, acc_ref):
        c = pl.program_id(1)
        step = pl.program_id(0) * nc + c

        def page_copy(st, sl, jj):
            page = pt_ref[st // nc, (st % nc) * pp + jj]
            return pltpu.make_async_copy(cache_ref.at[page, layer], buf.at[sl, jj], sem.at[sl])

        slot = _gather_pipeline(step, b * nc, page_copy, pp)

        @pl.when(c == 0)
        def _():
            m_ref[...] = jnp.full(m_ref.shape, -jnp.inf, F32)
            l_ref[...] = jnp.zeros(l_ref.shape, F32)
            acc_ref[...] = jnp.zeros(acc_ref.shape, F32)

        qb = q_ref[...].astype(BF16)
        kbs = [buf[slot, jj].astype(BF16) for jj in range(pp)]
        s = jnp.concatenate([jnp.dot(qb, kb, preferred_element_type=F32) for kb in kbs], axis=1) * scale
        m_new = jnp.maximum(m_ref[...], jnp.max(s, axis=-1, keepdims=True))
        alpha = jnp.exp(m_ref[...] - m_new)
        pr = jnp.exp(s - m_new)
        l_ref[...] = l_ref[...] * alpha + jnp.sum(pr, axis=-1, keepdims=True)
        pv = acc_ref[...] * alpha
        for jj, kb in enumerate(kbs):
            pv = pv + _bdot_nt(pr[:, jj * PAGE_SIZE:(jj + 1) * PAGE_SIZE], kb[0:MLA_KV_LORA, :])
        acc_ref[...] = pv
        m_ref[...] = m_new

        @pl.when(c == nc - 1)
        def _():
            kn = kn_ref[...]
            sn = jnp.sum(q_ref[...] * kn, axis=-1, keepdims=True) * scale
            m2 = jnp.maximum(m_ref[...], sn)
            a2 = jnp.exp(m_ref[...] - m2)
            pn = jnp.exp(sn - m2)
            l2 = l_ref[...] * a2 + pn
            o_ref[...] = (acc_ref[...] * a2 + pn * kn[:, 0:MLA_KV_LORA]) / l2

    gs = pltpu.PrefetchScalarGridSpec(
        num_scalar_prefetch=1, grid=(b, nc),
        in_specs=[pl.BlockSpec((None, 8, MLA_LAT), lambda i, c, pt: (i, 0, 0)),
                  pl.BlockSpec((None, 1, MLA_LAT), lambda i, c, pt: (i, 0, 0)), pl.BlockSpec(memory_space=pl.ANY)],
        out_specs=pl.BlockSpec((None, 8, MLA_KV_LORA), lambda i, c, pt: (i, 0, 0)),
        scratch_shapes=[pltpu.VMEM((2, pp, MLA_LAT, PAGE_SIZE), F32), pltpu.SemaphoreType.DMA((2,)),
                        pltpu.VMEM((8, 1), F32), pltpu.VMEM((8, 1), F32), pltpu.VMEM((8, MLA_KV_LORA), F32)])
    return pl.pallas_call(body, grid_spec=gs, out_shape=jax.ShapeDtypeStruct((b, 8, MLA_KV_LORA), F32),
                          compiler_params=_cparams(("arbitrary", "arbitrary")), name="mla_decode")(
        page_table, q8, knew, cache)


NSA_P_COLS = NSA_WIDTH + 7 * LANES
KV_W = 2 * NSA_HEAD_DIM


def _nsa_prep(p, g_q, g_ksel, g_kwin, tm, col_blk=0):
    m = p.shape[0]
    dh = NSA_HEAD_DIM

    def body(p_ref, gq_ref, gs_ref, gw_ref, q_ref, c_ref, s_ref, w_ref, g_ref):
        for h in range(NSA_HEADS):
            q_ref[:, h * dh:(h + 1) * dh] = _rms(p_ref[:, h * dh:(h + 1) * dh], gq_ref[...])
        o = NSA_WIDTH
        c_ref[...] = p_ref[:, o:o + 2 * dh]
        s_ref[:, 0:dh] = _rms(p_ref[:, o + 2 * dh:o + 3 * dh], gs_ref[...])
        s_ref[:, dh:2 * dh] = p_ref[:, o + 3 * dh:o + 4 * dh]
        w_ref[:, 0:dh] = _rms(p_ref[:, o + 4 * dh:o + 5 * dh], gw_ref[...])
        w_ref[:, dh:2 * dh] = p_ref[:, o + 5 * dh:o + 6 * dh]
        g_ref[...] = _sigmoid(p_ref[:, o + 6 * dh:o + 7 * dh])

    row = lambda w: pl.BlockSpec((tm, w), lambda i: (i, 0))
    vec = pl.BlockSpec((1, dh), lambda i: (0, 0))
    return pl.pallas_call(
        body, grid=(m // tm,), in_specs=[pl.BlockSpec((tm, NSA_P_COLS), lambda i: (i, col_blk)), vec, vec, vec],
        out_specs=[row(NSA_WIDTH), row(KV_W), row(KV_W), row(KV_W), row(LANES)],
        out_shape=[jax.ShapeDtypeStruct((m, w), F32) for w in (NSA_WIDTH, KV_W, KV_W, KV_W, LANES)],
        compiler_params=_cparams(("arbitrary",)), name="nsa_prep")(p, g_q, g_ksel, g_kwin)


def _gelu_tanh(x):
    return 0.5 * x * (1.0 + jnp.tanh(math.sqrt(2.0 / math.pi) * (x + 0.044715 * x * x * x)))


def _compress_tail(acc_k, acc_v, w2k_ref, w2v_ref, gk_ref):
    kc = _rms(_bdot(_gelu_tanh(acc_k), w2k_ref[...]), gk_ref[...])
    vc = _bdot(_gelu_tanh(acc_v), w2v_ref[...])
    return kc, vc


def _nsa_compress(xb, pe, w1k, w1v, w2k, w2v, g_k):
    nb = xb.shape[0]
    bm = min(nb, 256)
    dh = NSA_HEAD_DIM

    def body(x_ref, pe_ref, w1k_ref, w1v_ref, w2k_ref, w2v_ref, gk_ref, kc_ref, vc_ref):
        x = x_ref[...] + pe_ref[...]
        xk = jnp.concatenate([x[:, r * KV_W:r * KV_W + dh] for r in range(CMP_BLOCK)], axis=1)
        xv = jnp.concatenate([x[:, r * KV_W + dh:(r + 1) * KV_W] for r in range(CMP_BLOCK)], axis=1)
        kc, vc = _compress_tail(_bdot(xk, w1k_ref[...]), _bdot(xv, w1v_ref[...]), w2k_ref, w2v_ref, gk_ref)
        kc_ref[...] = kc
        vc_ref[...] = vc

    full = lambda a: pl.BlockSpec(a.shape, lambda i: (0,) * a.ndim)
    return pl.pallas_call(
        body, grid=(nb // bm,),
        in_specs=[pl.BlockSpec((bm, CMP_BLOCK * KV_W), lambda i: (i, 0)), full(pe), full(w1k), full(w1v), full(w2k),
                  full(w2v), full(g_k)],
        out_specs=[pl.BlockSpec((bm, dh), lambda i: (i, 0)), pl.BlockSpec((bm, dh), lambda i: (i, 0))],
        out_shape=[jax.ShapeDtypeStruct((nb, dh), F32), jax.ShapeDtypeStruct((nb, dh), F32)],
        compiler_params=_cparams(("arbitrary",)), name="nsa_compress")(xb, pe, w1k, w1v, w2k, w2v, g_k)


CMP_PAGES = 64
BLOCKS_PER_PAGE = PAGE_SIZE // CMP_BLOCK
PAGE_PITCH = PAGE_SIZE + 8


def _nsa_compress_paged(cache, page_table, layer, pe2, w1k, w1v, w2k, w2v, g_k):
    b, n_pages = page_table.shape
    pp = min(CMP_PAGES, n_pages)
    nc = n_pages // pp
    nblk = pp * BLOCKS_PER_PAGE
    dh = NSA_HEAD_DIM
    n_steps = b * nc

    def body(pt_ref, cache_ref, pe_ref, w1k_ref, w1v_ref, w2k_ref, w2v_ref, gk_ref, kc_ref, vc_ref, buf, sem):
        step = pl.program_id(0) * nc + pl.program_id(1)

        def page_copy(st, sl, k):
            jj, half = k // 2, k % 2
            page = pt_ref[st // nc, (st % nc) * pp + jj]
            return pltpu.make_async_copy(cache_ref.at[page, layer, :, pl.ds(half * dh, dh)],
                                         buf.at[sl, half, pl.ds(jj * PAGE_PITCH, PAGE_SIZE)], sem.at[sl])

        slot = _gather_pipeline(step, n_steps, page_copy, 2 * pp)

        def rows(half, r):
            per_q = [buf[slot, half, pl.ds(q * CMP_BLOCK + r, pp, stride=PAGE_PITCH), :] for q in range(BLOCKS_PER_PAGE)]
            return jnp.concatenate(per_q, axis=0) + pe_ref[r:r + 1, half * dh:(half + 1) * dh]

        acc_k = jnp.zeros((nblk, CMP_HIDDEN), F32)
        acc_v = jnp.zeros((nblk, CMP_HIDDEN), F32)
        for r in range(0, CMP_BLOCK, 2):
            xk = jnp.concatenate([rows(0, r), rows(0, r + 1)], axis=1)
            xv = jnp.concatenate([rows(1, r), rows(1, r + 1)], axis=1)
            acc_k = acc_k + _bdot(xk, w1k_ref[r * dh:(r + 2) * dh, :])
            acc_v = acc_v + _bdot(xv, w1v_ref[r * dh:(r + 2) * dh, :])
        kc, vc = _compress_tail(acc_k, acc_v, w2k_ref, w2v_ref, gk_ref)
        for q in range(BLOCKS_PER_PAGE):
            kc_ref[pl.ds(q, pp, stride=BLOCKS_PER_PAGE), :] = kc[q * pp:(q + 1) * pp, :]
            vc_ref[pl.ds(q, pp, stride=BLOCKS_PER_PAGE), :] = vc[q * pp:(q + 1) * pp, :]

    full = lambda a: pl.BlockSpec(a.shape, lambda i, c, pt: (0,) * a.ndim)
    gs = pltpu.PrefetchScalarGridSpec(
        num_scalar_prefetch=1, grid=(b, nc),
        in_specs=[pl.BlockSpec(memory_space=pl.ANY), full(pe2), full(w1k), full(w1v), full(w2k), full(w2v), full(g_k)],
        out_specs=[pl.BlockSpec((None, nblk, dh), lambda i, c, pt: (i, c, 0)),
                   pl.BlockSpec((None, nblk, dh), lambda i, c, pt: (i, c, 0))],
        scratch_shapes=[pltpu.VMEM((2, 2, pp * PAGE_PITCH, dh), F32), pltpu.SemaphoreType.DMA((2,))])
    shp = jax.ShapeDtypeStruct((b, n_pages * BLOCKS_PER_PAGE, dh), F32)
    return pl.pallas_call(body, grid_spec=gs, out_shape=[shp, shp],
                          compiler_params=_cparams(("arbitrary", "arbitrary")), name="nsa_compress_paged")(
        page_table, cache, pe2, w1k, w1v, w2k, w2v, g_k)


def _masked_softmax(l, mask):
    l = jnp.where(mask, l, -jnp.inf)
    mx = jnp.max(l, axis=-1, keepdims=True)
    mx = jnp.where(mx == -jnp.inf, 0.0, mx)
    p = jnp.where(mask, jnp.exp(l - mx), 0.0)
    return p / jnp.maximum(jnp.sum(p, axis=-1, keepdims=True), 1e-30)


def _select_blocks(imp, n_lanes, n_sel):
    lane = lax.broadcasted_iota(I32, imp.shape, 1)
    picked = jnp.zeros(imp.shape, F32)
    ids = []
    for _ in range(n_sel):
        mx = jnp.max(imp, axis=-1, keepdims=True)
        idx = jnp.min(jnp.where(imp == mx, lane, n_lanes), axis=-1, keepdims=True)
        hit = lane == idx
        picked = jnp.where(hit, 1.0, picked)
        imp = jnp.where(hit, -2.0, imp)
        ids.append(idx)
    return picked, ids


def _stack_heads(x, nh, dh):
    return jnp.concatenate([x[:, h * dh:(h + 1) * dh] for h in range(nh)], axis=0)


def _nsa_cmp_select(qn, kcmp, vcmp, pair, rel_bias):
    t = qn.shape[0]
    nc, ns = pair.shape
    nh, dh, tq = NSA_HEADS, NSA_HEAD_DIM, Q_TILE
    n_sel = min(N_SEL, ns)
    scale = dh ** -0.5

    def body(rb_ref, q_ref, kc_ref, vc_ref, pair_ref, o_ref, sel_ref):
        t0 = pl.program_id(0) * tq
        q4 = _stack_heads(q_ref[...], nh, dh)
        lc = (_bdot_nt(q4, kc_ref[...]) * scale).reshape(nh, tq, nc)
        tpos = t0 + lax.broadcasted_iota(I32, (tq, nc), 0)
        dist = tpos - (lax.broadcasted_iota(I32, (tq, nc), 1) * CMP_BLOCK + CMP_BLOCK - 1)
        lc = lc + jnp.stack([_t5_bias(dist, rb_ref, h) for h in range(nh)], axis=0)
        pc = _masked_softmax(lc, (dist >= 0)[None])
        oc = _bdot(pc.reshape(nh * tq, nc), vc_ref[...])
        for h in range(nh):
            o_ref[:, h * dh:(h + 1) * dh] = oc[h * tq:(h + 1) * tq, :]
        imp = jnp.dot(jnp.sum(pc, axis=0), pair_ref[...], precision=HI, preferred_element_type=F32)
        blk = lax.broadcasted_iota(I32, (tq, ns), 1)
        tq_pos = t0 + lax.broadcasted_iota(I32, (tq, ns), 0)
        cur = tq_pos // SEL_BLOCK
        forced = (blk == 0) | (blk == cur) | (blk == cur - 1)
        valid = blk * SEL_BLOCK <= tq_pos
        imp = jnp.where(forced, jnp.inf, jnp.where(valid, imp, -1.0))
        sel_ref[...] = _select_blocks(imp, ns, n_sel)[0]

    gs = pltpu.PrefetchScalarGridSpec(
        num_scalar_prefetch=0, grid=(t // tq,),
        in_specs=[pl.BlockSpec(memory_space=pltpu.SMEM),
                  pl.BlockSpec((tq, NSA_WIDTH), lambda i: (i, 0)), pl.BlockSpec((nc, dh), lambda i: (0, 0)),
                  pl.BlockSpec((nc, dh), lambda i: (0, 0)), pl.BlockSpec((nc, ns), lambda i: (0, 0))],
        out_specs=[pl.BlockSpec((tq, NSA_WIDTH), lambda i: (i, 0)), pl.BlockSpec((tq, ns), lambda i: (i, 0))])
    return pl.pallas_call(body, grid_spec=gs,
                          out_shape=[jax.ShapeDtypeStruct((t, NSA_WIDTH), F32), jax.ShapeDtypeStruct((t, ns), F32)],
                          compiler_params=_cparams(("arbitrary",)), name="nsa_cmp_select")(rel_bias, qn, kcmp, vcmp, pair)


def _n_bias_tiles(tq, tk):
    d = 0
    while d * tq - (tk - 1) < T5_THR[-1]:
        d += 1
    return d + 1


def _nsa_flash(qn, gates, o_cmp, selmask, sel_kv, win_kv, rel_bias):
    t = qn.shape[0]
    ns = selmask.shape[1]
    nh, dh, tq, tk = NSA_HEADS, NSA_HEAD_DIM, min(FLASH_Q_TILE, t), K_TILE
    n_bias = _n_bias_tiles(tq, tk)
    scale = dh ** -0.5
    spb = tk // SEL_BLOCK
    qi, kj = _causal_schedule(t // tq, tq, tk)

    def body(qi_ref, kj_ref, rb_ref, q_ref, g_ref, oc_ref, sm_ref, sk_ref, wk_ref, o_ref, bt_ref, qs_ref,
             ms, ls, accs, mw, lw, accw):
        i, j = qi_ref[pl.program_id(0)], kj_ref[pl.program_id(0)]
        t0, k0 = i * tq, j * tk
        rel = lax.broadcasted_iota(I32, (tq, tk), 0) - lax.broadcasted_iota(I32, (tq, tk), 1)

        @pl.when(pl.program_id(0) == 0)
        def _():
            for d in range(n_bias):
                for h in range(nh):
                    bt_ref[d, h] = _t5_bias(rel + d * tq, rb_ref, h)

        @pl.when(j == 0)
        def _():
            qs_ref[...] = (_stack_heads(q_ref[...], nh, dh) * scale).astype(BF16)
            for m_ref, l_ref, a_ref in ((ms, ls, accs), (mw, lw, accw)):
                m_ref[...] = jnp.full(m_ref.shape, -jnp.inf, F32)
                l_ref[...] = jnp.zeros(l_ref.shape, F32)
                a_ref[...] = jnp.zeros(a_ref.shape, F32)

        def update(m_ref, l_ref, a_ref, keys, bias, admit, vals):
            kb, vb = keys.astype(BF16), vals.astype(BF16)
            for h in range(nh):
                rows = pl.ds(h * tq, tq)
                s = _bdot_nt(qs_ref[rows, :], kb) + bias[h] + admit
                m_old = m_ref[rows, :]
                m_new = jnp.maximum(m_old, jnp.max(s, axis=-1, keepdims=True))
                m_safe = jnp.where(m_new == -jnp.inf, 0.0, m_new)
                alpha = jnp.exp(m_old - m_safe)
                pr = jnp.exp(s - m_safe)
                l_ref[rows, :] = l_ref[rows, :] * alpha + jnp.sum(pr, axis=-1, keepdims=True)
                a_ref[rows, :] = a_ref[rows, :] * alpha + _bdot(pr, vb)
                m_ref[rows, :] = m_new

        bias = bt_ref[jnp.minimum((t0 - k0) // tq, n_bias - 1)]
        dist = (t0 - k0) + rel
        expand = (lax.broadcasted_iota(I32, (ns, tk), 0) == j * spb + lax.broadcasted_iota(I32, (ns, tk), 1) // SEL_BLOCK)
        chosen = _bdot(sm_ref[...], expand.astype(F32)) > 0.5
        skv = sk_ref[...]
        update(ms, ls, accs, skv[:, 0:dh], bias, jnp.where(chosen & (dist >= 0), 0.0, -jnp.inf), skv[:, dh:KV_W])

        @pl.when(k0 + tk - 1 >= t0 - NSA_WINDOW)
        def _():
            wkv = wk_ref[...]
            update(mw, lw, accw, wkv[:, 0:dh], bias, jnp.where((dist >= 0) & (dist <= NSA_WINDOW), 0.0, -jnp.inf),
                   wkv[:, dh:KV_W])

        @pl.when(j == (t0 + tq - 1) // tk)
        def _():
            o_s = accs[...] / jnp.maximum(ls[...], 1e-30)
            o_w = accw[...] / jnp.maximum(lw[...], 1e-30)
            g = g_ref[...]
            for h in range(nh):
                rows = slice(h * tq, (h + 1) * tq)
                o_ref[:, h * dh:(h + 1) * dh] = (g[:, 3 * h:3 * h + 1] * oc_ref[:, h * dh:(h + 1) * dh]
                                                 + g[:, 3 * h + 1:3 * h + 2] * o_s[rows]
                                                 + g[:, 3 * h + 2:3 * h + 3] * o_w[rows])

    qrow = lambda w: pl.BlockSpec((tq, w), lambda s, qi, kj: (qi[s], 0))
    krow = pl.BlockSpec((tk, KV_W), lambda s, qi, kj: (kj[s], 0))
    gs = pltpu.PrefetchScalarGridSpec(
        num_scalar_prefetch=2, grid=(qi.shape[0],),
        in_specs=[pl.BlockSpec(memory_space=pltpu.SMEM), qrow(NSA_WIDTH), qrow(LANES), qrow(NSA_WIDTH), qrow(ns), krow, krow],
        out_specs=qrow(NSA_WIDTH),
        scratch_shapes=[pltpu.VMEM((n_bias, nh, tq, tk), F32), pltpu.VMEM((nh * tq, dh), BF16)]
        + [pltpu.VMEM((nh * tq, 1), F32), pltpu.VMEM((nh * tq, 1), F32), pltpu.VMEM((nh * tq, dh), F32)] * 2)
    return pl.pallas_call(body, grid_spec=gs, out_shape=jax.ShapeDtypeStruct((t, NSA_WIDTH), F32),
                          compiler_params=_cparams(("arbitrary",)), name="nsa_flash")(
        qi, kj, rel_bias, qn, gates, o_cmp, selmask, sel_kv, win_kv)


def _head_bias_rows(dist, rb_ref):
    n = dist.shape[1]
    return jnp.concatenate([_t5_bias(dist, rb_ref, h) for h in range(NSA_HEADS)]
                           + [jnp.zeros((8 - NSA_HEADS, n), F32)], axis=0)


def _nsa_cmp_select_decode(q8, kcmp, vcmp, pair, rel_bias, past):
    b, ncp, dh = kcmp.shape
    nsp = pair.shape[1]
    ns = (past + 1 + SEL_BLOCK - 1) // SEL_BLOCK
    ns_l = -(-ns // LANES) * LANES
    n_sel = min(N_SEL, ns)
    scale = dh ** -0.5
    sb = math.gcd(b, 8)
    assert past % PAGE_SIZE == 0 and ncp == past // CMP_BLOCK and nsp == past // SEL_BLOCK and ns_l > nsp

    def body(rb_ref, q_ref, kc_ref, vc_ref, pair_ref, o_ref, id_ref):
        dist = past - (lax.broadcasted_iota(I32, (1, ncp), 1) * CMP_BLOCK + CMP_BLOCK - 1)
        bias = _head_bias_rows(dist, rb_ref)
        mask = jnp.broadcast_to(dist >= 0, (8, ncp))
        head_sums = []
        for s in range(sb):
            pc = _masked_softmax(_bdot_nt(q_ref[s], kc_ref[s]) * scale + bias, mask)
            o_ref[s] = _bdot(pc, vc_ref[s])
            head_sums.append(jnp.sum(pc[0:NSA_HEADS], axis=0, keepdims=True))
        imp = jnp.dot(jnp.concatenate(head_sums, axis=0), pair_ref[...], precision=HI,
                      preferred_element_type=F32)
        imp = jnp.concatenate([imp, jnp.zeros((sb, ns_l - nsp), F32)], axis=1)
        blk = lax.broadcasted_iota(I32, (sb, ns_l), 1)
        cur = past // SEL_BLOCK
        forced = (blk == 0) | (blk == cur) | (blk == cur - 1)
        valid = blk * SEL_BLOCK <= past
        imp = jnp.where(blk < ns, jnp.where(forced, jnp.inf, jnp.where(valid, imp, -1.0)), -2.0)
        ids = _select_blocks(imp, ns_l, n_sel)[1]
        lane = lax.broadcasted_iota(I32, (sb, LANES), 1)
        out = jnp.zeros((sb, LANES), I32)
        for r, idx in enumerate(ids):
            out = jnp.where(lane == r, idx, out)
        id_ref[...] = out

    per = lambda r: pl.BlockSpec((sb, r, dh), lambda i: (i, 0, 0))
    gs = pltpu.PrefetchScalarGridSpec(
        num_scalar_prefetch=0, grid=(b // sb,),
        in_specs=[pl.BlockSpec(memory_space=pltpu.SMEM), per(8), per(ncp), per(ncp),
                  pl.BlockSpec((ncp, nsp), lambda i: (0, 0))],
        out_specs=[per(8), pl.BlockSpec((sb, LANES), lambda i: (i, 0))])
    return pl.pallas_call(body, grid_spec=gs,
                          out_shape=[jax.ShapeDtypeStruct((b, 8, dh), F32), jax.ShapeDtypeStruct((b, LANES), I32)],
                          compiler_params=_cparams(("arbitrary",)), name="nsa_cmp_select_decode")(
        rel_bias, q8, kcmp, vcmp, pair)


def _nsa_sel_win_decode(q8, gates, o_cmp, new_sel, new_win, sel_ids, page_table, cache_sel, win_state, rel_bias, layer, past,
                        earlier):
    b, n_pages = page_table.shape
    dh = NSA_HEAD_DIM
    n_sel = min(N_SEL, (past + 1 + SEL_BLOCK - 1) // SEL_BLOCK)
    n_cached = past // SEL_BLOCK
    halves = PAGE_SIZE // SEL_BLOCK
    wr = win_state.shape[2]
    keep = min(NSA_WINDOW, wr + 1)
    scale = dh ** -0.5
    assert wr == min(NSA_WINDOW, past) and keep == wr

    def body(ids_ref, pt_ref, rb_ref, q_ref, g_ref, oc_ref, ns_ref, nw_ref, ws_ref, cache_ref, *rest):
        early, (o_ref, nwin_ref, buf, sem) = rest[:len(earlier)], rest[len(earlier):]
        i = pl.program_id(0)
        if earlier:
            nwin_ref[0:layer] = early[0][...]

        def block_copy(st, sl, r):
            bid = jnp.minimum(ids_ref[st, r], n_cached - 1)
            return pltpu.make_async_copy(
                cache_ref.at[pt_ref[st, bid // halves], layer, pl.ds((bid % halves) * SEL_BLOCK, SEL_BLOCK)],
                buf.at[sl, r], sem.at[sl])

        slot = _gather_pipeline(i, b, block_copy, n_sel)
        q = q_ref[...]
        zero = jnp.zeros((1, 1), I32)

        def attend(parts, new_row, new_penalty):
            sn = jnp.sum(q * new_row[:, 0:dh], axis=-1, keepdims=True) * scale + _head_bias_rows(zero, rb_ref)
            sn = sn + new_penalty
            masked = [jnp.where(v, s, -jnp.inf) for s, v, _ in parts]
            mx = sn
            for s in masked:
                mx = jnp.maximum(mx, jnp.max(s, axis=-1, keepdims=True))
            mx = jnp.where(mx == -jnp.inf, 0.0, mx)
            pn = jnp.exp(sn - mx)
            den = pn
            num = pn * new_row[:, dh:KV_W]
            for s, (_, _, vals) in zip(masked, parts):
                pr = jnp.exp(s - mx)
                den = den + jnp.sum(pr, axis=-1, keepdims=True)
                num = num + _bdot(pr, vals)
            return num / jnp.maximum(den, 1e-30)

        lane_b = lax.broadcasted_iota(I32, (1, SEL_BLOCK), 1)
        parts = []
        n_cur = 0
        for r in range(n_sel):
            bid = ids_ref[i, r]
            kv = buf[slot, r]
            dist = past - (bid * SEL_BLOCK + lane_b) - jnp.where(bid < n_cached, 0, 2 * (past + SEL_BLOCK))
            s = _bdot_nt(q, kv[:, 0:dh]) * scale + _head_bias_rows(dist, rb_ref)
            parts.append((s, dist >= 0, kv[:, dh:KV_W]))
            n_cur = n_cur + jnp.where(bid == past // SEL_BLOCK, 1, 0)
        o_s = attend(parts, ns_ref[...], jnp.where(n_cur > 0, 0.0, -jnp.inf))

        wkv = ws_ref[...]
        dist_w = wr - lax.broadcasted_iota(I32, (1, wr), 1)
        sw = _bdot_nt(q, wkv[:, 0:dh]) * scale + _head_bias_rows(dist_w, rb_ref)
        o_w = attend([(sw, (dist_w >= 0) & (dist_w <= NSA_WINDOW), wkv[:, dh:KV_W])], nw_ref[...], 0.0)
        nwin_ref[layer, 0:keep - 1, :] = ws_ref[wr + 1 - keep:wr, :]
        nwin_ref[layer, keep - 1:keep, :] = nw_ref[...]

        g = jnp.broadcast_to(g_ref[...], (8, LANES))
        lane = lax.broadcasted_iota(I32, (8, LANES), 1)
        row = lax.broadcasted_iota(I32, (8, LANES), 0)
        gcol = [jnp.sum(jnp.where(lane == 3 * row + k, g, 0.0), axis=-1, keepdims=True) for k in range(3)]
        o_ref[...] = gcol[0] * oc_ref[...] + gcol[1] * o_s + gcol[2] * o_w

    per = lambda r, w: pl.BlockSpec((None, r, w), lambda i, ids, pt: (i, 0, 0))
    gs = pltpu.PrefetchScalarGridSpec(
        num_scalar_prefetch=2, grid=(b,),
        in_specs=[pl.BlockSpec(memory_space=pltpu.SMEM), per(8, dh), per(1, LANES), per(8, dh), per(1, KV_W), per(1, KV_W),
                  pl.BlockSpec((None, None, wr, KV_W), lambda i, ids, pt: (i, layer, 0, 0)),
                  pl.BlockSpec(memory_space=pl.ANY)]
        + [pl.BlockSpec((None, layer, keep, KV_W), lambda i, ids, pt: (i, 0, 0, 0)) for _ in earlier],
        out_specs=[per(8, dh), pl.BlockSpec((None, layer + 1, keep, KV_W), lambda i, ids, pt: (i, 0, 0, 0))],
        scratch_shapes=[pltpu.VMEM((2, n_sel, SEL_BLOCK, KV_W), F32), pltpu.SemaphoreType.DMA((2,))])
    return pl.pallas_call(body, grid_spec=gs,
                          out_shape=[jax.ShapeDtypeStruct((b, 8, dh), F32),
                                     jax.ShapeDtypeStruct((b, layer + 1, keep, KV_W), F32)],
                          compiler_params=_cparams(("arbitrary",)), name="nsa_sel_win_decode")(
        sel_ids, page_table, rel_bias, q8, gates, o_cmp, new_sel, new_win, win_state, cache_sel, *earlier)


def _pair_matrix(nc, ns):
    r = SEL_BLOCK // CMP_BLOCK
    return (jnp.arange(nc)[:, None] // r == jnp.arange(ns)[None, :]).astype(F32)


def _nsa_weights(q_norm, k_norm, pe, w1, w2):
    pe2 = jnp.concatenate([pe[0], pe[1]], axis=1)
    return (q_norm[None], k_norm[0][None], k_norm[1][None], k_norm[2][None], pe2,
            w1[0].astype(BF16), w1[1].astype(BF16), w2[0].astype(BF16), w2[1].astype(BF16))


def _nsa_prompt(p, nw, rel_bias, col_blk=0):
    g_q, g_kc, g_ks, g_kw, pe2, w1k, w1v, w2k, w2v = nw
    t = p.shape[0]
    qn, cmp_rows, sel_rows, win_rows, gates = _nsa_prep(p, g_q, g_ks, g_kw, min(t, 512), col_blk)
    nc, ns = t // CMP_BLOCK, t // SEL_BLOCK
    kcmp, vcmp = _nsa_compress(cmp_rows.reshape(nc, CMP_BLOCK * KV_W), pe2.reshape(1, CMP_BLOCK * KV_W),
                               w1k, w1v, w2k, w2v, g_kc)
    o_cmp, selmask = _nsa_cmp_select(qn, kcmp, vcmp, _pair_matrix(nc, ns), rel_bias)
    o = _nsa_flash(qn, gates, o_cmp, selmask, sel_rows, win_rows, rel_bias)
    return o, cmp_rows, sel_rows, win_rows


def _nsa_decode(p, cache_cmp, cache_sel, win_state, page_table, nw, rel_bias, layer, col_blk=0, earlier_win=()):
    g_q, g_kc, g_ks, g_kw, pe2, w1k, w1v, w2k, w2v = nw
    b, n_pages = page_table.shape
    past = n_pages * PAGE_SIZE
    qn, cmp_rows, sel_rows, win_rows, gates = _nsa_prep(p, g_q, g_ks, g_kw, b, col_blk)
    q8 = jnp.pad(qn.reshape(b, NSA_HEADS, NSA_HEAD_DIM), ((0, 0), (0, 8 - NSA_HEADS), (0, 0)))
    kcmp, vcmp = _nsa_compress_paged(cache_cmp, page_table, layer, pe2, w1k, w1v, w2k, w2v, g_kc)
    o_cmp, ids = _nsa_cmp_select_decode(q8, kcmp, vcmp, _pair_matrix(past // CMP_BLOCK, past // SEL_BLOCK), rel_bias, past)
    o8, new_win = _nsa_sel_win_decode(q8, gates[:, None, :], o_cmp, sel_rows[:, None, :], win_rows[:, None, :],
                                      ids[:, :N_SEL], page_table, cache_sel, win_state, rel_bias, layer, past,
                                      earlier_win)
    return o8[:, :NSA_HEADS].reshape(b, NSA_WIDTH), cmp_rows, sel_rows, new_win


def _moe(x, g, wr, br, w_gate, w_up, w_down, layer, tm_route, tm):
    h, route = _moe_router(x, g, wr, br, tm_route)
    eid = route[:, 0:2].astype(I32)
    wts = route[:, 2:4]
    row_src, row_w, slot_dest, tile_e, n_valid = _moe_dispatch(eid, wts, tm)
    ys = _moe_grouped(h, row_src, row_w[:, None], tile_e, n_valid, w_gate, w_up, w_down, layer, tm)
    return _moe_combine(x, ys, slot_dest, min(x.shape[0], 128))


P_COLS = 9 * MLA_P_COLS
NSA_COL_BLK = GDN_P_COLS // NSA_P_COLS
MLA_COL_BLK = P_COLS // MLA_P_COLS - 1
assert GDN_P_COLS % NSA_P_COLS == 0 and (NSA_COL_BLK + 1) * NSA_P_COLS <= MLA_COL_BLK * MLA_P_COLS


def _fused_w_in(w_in):
    n_gdn = 4 * GDN_WIDTH + 2 * GDN_HEADS
    n_nsa = NSA_WIDTH + 6 * NSA_HEAD_DIM + 3 * NSA_HEADS
    n_mla = MLA_Q_LORA + MLA_KV_LORA + MLA_ROPE
    assert w_in.shape[0] == n_gdn + n_nsa + n_mla
    rows = lambda a, b: w_in[a:b].astype(BF16)
    zeros = lambda n: jnp.zeros((n, w_in.shape[1]), BF16)
    o = MLA_COL_BLK * MLA_P_COLS
    return jnp.concatenate([rows(0, n_gdn), zeros(GDN_P_COLS - n_gdn), rows(n_gdn, n_gdn + n_nsa),
                            zeros(o - GDN_P_COLS - n_nsa), rows(n_gdn + n_nsa, n_gdn + n_nsa + n_mla),
                            zeros(P_COLS - o - n_mla)], axis=0)


def _layer_weights(l, norm1, w_in, gdn_conv, gdn_a_log, gdn_dt_bias, gdn_out_norm, nsa_q_norm, nsa_k_norm, nsa_cmp_pe,
                   nsa_cmp_w1, nsa_cmp_w2, mla_q_a_norm, mla_w_uq, mla_qk_norm, mla_kv_norm, mla_krope_norm, mla_w_uk,
                   mla_w_uv, w_out, norm2, moe_w_grp, moe_b_grp, moe_w_exp, moe_b_exp):
    lane8 = lambda v: jnp.zeros((1, LANES), F32).at[0, GDN_HEADS:2 * GDN_HEADS].set(v)
    wr = jnp.zeros((D_MODEL, LANES), F32).at[:, :N_GROUPS].set(moe_w_grp[l]).at[:, N_GROUPS:N_GROUPS + N_EXPERTS].set(moe_w_exp[l])
    br = jnp.zeros((1, LANES), F32).at[0, :N_GROUPS].set(moe_b_grp[l]).at[0, N_GROUPS:N_GROUPS + N_EXPERTS].set(moe_b_exp[l])
    return dict(
        norm1=norm1[l][None], w_in=_fused_w_in(jnp.transpose(w_in, (2, 0, 1))[:, l, :]),
        gdn=(gdn_conv[l], lane8(gdn_a_log[l]), lane8(gdn_dt_bias[l]), gdn_out_norm[l][None]),
        nsa=_nsa_weights(nsa_q_norm[l], nsa_k_norm[l], nsa_cmp_pe[l], nsa_cmp_w1[l], nsa_cmp_w2[l]),
        mla=_mla_weights(mla_q_a_norm[l], mla_w_uq[l], mla_qk_norm[l], mla_kv_norm[l], mla_krope_norm[l], mla_w_uk[l],
                         mla_w_uv[l]),
        w_out=w_out[l].astype(BF16), norm2=norm2[l][None], wr=wr, br=br)


def _prompt_layer(x, l, lw, rel_bias, moe_w):
    t = x.shape[0]
    p = _norm_matmul(x, lw["norm1"], lw["w_in"], 512, MLA_P_COLS)
    o_gdn, s_fin = _gdn_prompt(p, *lw["gdn"])
    new_conv = p[t - (CONV_W - 1):, 0:3 * GDN_WIDTH]
    o_nsa, cmp_rows, sel_rows, win_rows = _nsa_prompt(p, lw["nsa"], rel_bias, NSA_COL_BLK)
    cos, sin = _rope_tables(jnp.arange(t, dtype=I32))
    q, kv = _mla_prep(p, cos, sin, *lw["mla"][:-1], 512, MLA_COL_BLK)
    o_mla = _headwise_mm(_mla_flash(q, kv, 256, 512), lw["mla"][-1])
    x = _out_proj(x, o_gdn, o_nsa, o_mla, lw["w_out"], 512, 512)
    x = _moe(x, lw["norm2"], lw["wr"], lw["br"], *moe_w, l, 512, 512)
    return x, cmp_rows, sel_rows, kv, win_rows[t - min(NSA_WINDOW, t):], s_fin, new_conv


def _sample_layer(x, l, lw, rel_bias, moe_w, caches, states, page_table, earlier):
    b = x.shape[0]
    cache_cmp, cache_sel, cache_mla = caches
    win_state, gdn_state, conv_state = states
    past = page_table.shape[1] * PAGE_SIZE
    p = _norm_matmul(x, lw["norm1"], lw["w_in"], b, MLA_P_COLS)
    o_gdn, new_conv, new_gdn = _gdn_decode(p[:, None, :], conv_state, gdn_state, *lw["gdn"], l,
                                           (earlier[2], earlier[1]) if earlier else ())
    o_nsa, cmp_rows, sel_rows, new_win = _nsa_decode(p, cache_cmp, cache_sel, win_state, page_table, lw["nsa"], rel_bias,
                                                     l, NSA_COL_BLK, earlier[:1])
    cos, sin = _rope_tables(jnp.full((b,), past, I32))
    q, kv = _mla_prep(p, cos, sin, *lw["mla"][:-1], b, MLA_COL_BLK)
    q8 = jnp.pad(jnp.transpose(q, (1, 0, 2)), ((0, 0), (0, 8 - MLA_HEADS), (0, 0)))
    o_lat = _mla_decode(q8, kv[:, None, :], cache_mla, page_table, l)
    o_mla = _headwise_mm(jnp.transpose(o_lat[:, :MLA_HEADS], (1, 0, 2)), lw["mla"][-1])
    x = _out_proj(x, o_gdn[:, 0], o_nsa, o_mla, lw["w_out"], b, 512)
    x = _moe(x, lw["norm2"], lw["wr"], lw["br"], *moe_w, l, b, 32)
    return x, cmp_rows, sel_rows, kv, (new_win, new_gdn, new_conv)


def kernel(x_prompt, x_sample, cache_nsa_cmp, cache_nsa_sel, cache_mla, state_win_kv, state_gdn, state_conv, page_table, rel_bias, norm1, w_in, gdn_conv, gdn_a_log, gdn_dt_bias, gdn_out_norm, nsa_q_norm, nsa_k_norm, nsa_cmp_pe, nsa_cmp_w1, nsa_cmp_w2, mla_q_a_norm, mla_w_uq, mla_qk_norm, mla_kv_norm, mla_krope_norm, mla_w_uk, mla_w_uv, w_out, norm2, moe_w_grp, moe_b_grp, moe_w_exp, moe_b_exp, moe_w_gate, moe_w_up, moe_w_down):
    depth = norm1.shape[0]
    assert x_prompt.shape[0] == 1 and x_sample.shape[1] == 1
    moe_w = (moe_w_gate, moe_w_up, moe_w_down)
    lws = [_layer_weights(l, norm1, w_in, gdn_conv, gdn_a_log, gdn_dt_bias, gdn_out_norm, nsa_q_norm, nsa_k_norm,
                          nsa_cmp_pe, nsa_cmp_w1, nsa_cmp_w2, mla_q_a_norm, mla_w_uq, mla_qk_norm, mla_kv_norm,
                          mla_krope_norm, mla_w_uk, mla_w_uv, w_out, norm2, moe_w_grp, moe_b_grp, moe_w_exp, moe_b_exp)
           for l in range(depth)]

    xp = x_prompt[0]
    p_out = []
    for l in range(depth):
        xp, *rows = _prompt_layer(xp, l, lws[l], rel_bias, moe_w)
        p_out.append(rows)

    b = x_sample.shape[0]
    xs = x_sample[:, 0]
    conv_state = jnp.transpose(state_conv, (0, 2, 1, 3))
    cache_mla_t = jnp.transpose(cache_mla, (0, 1, 3, 2))
    s_out = []
    stacked = ()
    for l in range(depth):
        xs, *rows, stacked = _sample_layer(xs, l, lws[l], rel_bias, moe_w, (cache_nsa_cmp, cache_nsa_sel, cache_mla_t),
                                           (state_win_kv, state_gdn, conv_state), page_table, stacked)
        s_out.append(rows)
    s_win, s_gdn, s_conv = stacked

    def stack_p(k):
        return jnp.stack([p_out[l][k] for l in range(depth)], axis=0)[None]

    def stack_s(k, width):
        return jnp.stack([s_out[l][k] for l in range(depth)], axis=1).reshape(b, depth, 1, width)

    return (xp[None], xs[:, None],
            stack_p(0), stack_p(1), stack_p(2), stack_p(3), stack_p(4), stack_p(5),
            stack_s(0, KV_W), stack_s(1, KV_W), stack_s(2, MLA_LAT),
            s_win, s_gdn, jnp.transpose(s_conv, (0, 2, 1, 3)))
```
